```python
import math
import jax, jax.numpy as jnp
from jax import lax
import numpy as np

D_MODEL = 1024
BATCH = 4
SEQ = 8192
DEPTH = 1

GRID_W = 64
CTX_LEN = 256
EPS = 1e-6

SSD_HEADS = 16
SSD_HEAD_DIM = 64
SSD_INNER = SSD_HEADS * SSD_HEAD_DIM
SSD_GROUPS = 2
SSD_HPG = SSD_HEADS // SSD_GROUPS
SSD_STATE = 128
SSD_CONV = 3
SSD_CHUNK = 128
BC_WIDTH = SSD_GROUPS * SSD_STATE
XBC_WIDTH = SSD_INNER + 2 * BC_WIDTH

DA_HEADS = 4
DA_QK_DIM = 64
DA_V_DIM = 2 * DA_QK_DIM
DA_INNER = DA_HEADS * DA_V_DIM
Q_BLOCK = 128
ROPE_BASE = 10000.0
ROPE_AXIS_DIM = DA_QK_DIM // 2

MIX_WIDTH = SSD_INNER + DA_INNER
SPLITS = (SSD_INNER,
          SSD_INNER + XBC_WIDTH,
          SSD_INNER + XBC_WIDTH + 2 * SSD_HEADS,
          SSD_INNER + XBC_WIDTH + 2 * SSD_HEADS + DA_INNER,
          SSD_INNER + XBC_WIDTH + 2 * SSD_HEADS + 2 * DA_INNER)
IN_WIDTH = SPLITS[-1] + DA_INNER

D_FF = 2816
FFN_CONV = 3

kernel_name = 'hybrid_ssd_diffattn_dit_block'

F32 = jnp.float32


def rmsnorm(x, g):
    xf = x.astype(F32)
    y = xf * lax.rsqrt(jnp.mean(xf * xf, axis=-1, keepdims=True) + EPS)
    return (y * g.astype(F32)).astype(x.dtype)


def modulate(h, shift, scale):
    return h * (1 + scale) + shift


def dwconv_centred(x, w, b):
    k = w.shape[0]
    pad = k // 2
    n = x.shape[1]
    xp = jnp.pad(x, ((0, 0), (pad, pad), (0, 0)))
    out = b
    for j in range(k):
        out = out + xp[:, j:j + n] * w[j]
    return out


def axial_rope(row, col):
    inv = jnp.power(ROPE_BASE, -jnp.arange(0, ROPE_AXIS_DIM, 2, dtype=F32) / ROPE_AXIS_DIM)
    ar = row.astype(F32)[:, None] * inv
    ac = col.astype(F32)[:, None] * inv
    ang = jnp.concatenate([ar, ar, ac, ac], axis=-1)
    return jnp.cos(ang), jnp.sin(ang)


def _rot_half(u):
    u1, u2 = jnp.split(u, 2, axis=-1)
    return jnp.concatenate([-u2, u1], axis=-1)


def apply_rope(t, cos, sin):
    cos = cos[None, :, None, None, :].astype(t.dtype)
    sin = sin[None, :, None, None, :].astype(t.dtype)
    r = jnp.concatenate([_rot_half(t[..., :ROPE_AXIS_DIM]), _rot_half(t[..., ROPE_AXIS_DIM:])], axis=-1)
    return t * cos + r * sin


def diff_attention(q, k, v, lam):
    s = jnp.einsum('bqhcd,bkhcd->bhcqk', q.astype(F32), k.astype(F32)) * (DA_QK_DIM ** -0.5)
    p = jax.nn.softmax(s, axis=-1)
    w = p[:, :, 0] - lam * p[:, :, 1]
    return jnp.einsum('bhqk,bkhe->bqhe', w, v.astype(F32))


def diff_attention_blocks(q, k, v, lam):
    b, n = q.shape[:2]
    nb = n // Q_BLOCK
    qb = q.reshape(b, nb, Q_BLOCK, DA_HEADS, 2, DA_QK_DIM).swapaxes(0, 1)
    ob = lax.map(lambda qi: diff_attention(qi, k, v, lam), qb)
    return ob.swapaxes(0, 1).reshape(b, n, DA_HEADS, DA_V_DIM)


def _chunk(t):
    return t.reshape(t.shape[0], t.shape[1] // SSD_CHUNK, SSD_CHUNK, *t.shape[2:])


def ssd_inputs(xbc, dt_raw, dt_bias_d, reverse):
    b, n = xbc.shape[:2]
    xf = xbc.astype(F32)
    xs = xf[..., :SSD_INNER].reshape(b, n, SSD_GROUPS, SSD_HPG, SSD_HEAD_DIM)
    bm = xf[..., SSD_INNER:SSD_INNER + BC_WIDTH].reshape(b, n, SSD_GROUPS, SSD_STATE)
    cm = xf[..., SSD_INNER + BC_WIDTH:].reshape(b, n, SSD_GROUPS, SSD_STATE)
    dt = jax.nn.softplus(dt_raw.astype(F32) + dt_bias_d.astype(F32)).reshape(b, n, SSD_GROUPS, SSD_HPG)
    parts = (xs, dt, bm, cm)
    if reverse:
        parts = tuple(jnp.flip(t, axis=1) for t in parts)
    return tuple(_chunk(t) for t in parts)


def ssd_chunk_states(xs, dt, bm, a_cum, h0):
    decay_to_end = jnp.exp(a_cum[:, :, -1:] - a_cum)
    chunk_states = jnp.einsum('bclgn,bclgr,bclgrp->bcgrpn', bm, decay_to_end * dt, xs)
    chunk_decay = jnp.exp(a_cum[:, :, -1])

    def step(h, inp):
        s, d = inp
        return d[..., None, None] * h + s, h

    h_final, h_starts = lax.scan(step, h0, (chunk_states.swapaxes(0, 1), chunk_decay.swapaxes(0, 1)))
    return h_starts.swapaxes(0, 1), h_final


def ssd_chunk_outputs(xs, dt, bm, cm, a_cum, h_starts):
    lc = a_cum.shape[2]
    causal = jnp.tril(jnp.ones((lc, lc), dtype=bool))[None, None, :, :, None, None]
    seg = a_cum[:, :, :, None] - a_cum[:, :, None, :]
    decay = jnp.exp(jnp.where(causal, seg, -jnp.inf))
    scores = jnp.einsum('bclgn,bcsgn->bclsg', cm, bm)
    w = scores[..., None] * decay * dt[:, :, None]
    y_diag = jnp.einsum('bclsgr,bcsgrp->bclgrp', w, xs)
    y_off = jnp.einsum('bclgn,bcgrpn,bclgr->bclgrp', cm, h_starts, jnp.exp(a_cum))
    return y_diag + y_off


def _ssd_finish(y, reverse):
    b = y.shape[0]
    y = y.reshape(b, -1, SSD_INNER)
    return jnp.flip(y, axis=1) if reverse else y


def ssd_bidir(xbc_l, dt_l, xbc_c, dt_c, a_log, dt_bias, d_skip, need_ctx):
    b = xbc_l.shape[0]
    y_l, y_c = None, None
    for d in range(2):
        rev = d == 1
        a = -jnp.exp(a_log[d].astype(F32)).reshape(SSD_GROUPS, SSD_HPG)
        dsk = d_skip[d].astype(F32).reshape(SSD_GROUPS, SSD_HPG, 1)
        cols = slice(d * SSD_HEADS, (d + 1) * SSD_HEADS)
        xs_c, dtc, b_c, c_c = ssd_inputs(xbc_c, dt_c[..., cols], dt_bias[d], rev)
        acum_c = jnp.cumsum(dtc * a, axis=2)
        h0 = jnp.zeros((b, SSD_GROUPS, SSD_HPG, SSD_HEAD_DIM, SSD_STATE), F32)
        hs_c, h_ctx = ssd_chunk_states(xs_c, dtc, b_c, acum_c, h0)
        xs_l, dtl, b_l, c_l = ssd_inputs(xbc_l, dt_l[..., cols], dt_bias[d], rev)
        acum_l = jnp.cumsum(dtl * a, axis=2)
        hs_l, _ = ssd_chunk_states(xs_l, dtl, b_l, acum_l, h_ctx)
        yd = _ssd_finish(ssd_chunk_outputs(xs_l, dtl, b_l, c_l, acum_l, hs_l) + dsk * xs_l, rev)
        y_l = yd if y_l is None else y_l + yd
        if need_ctx:
            ydc = _ssd_finish(ssd_chunk_outputs(xs_c, dtc, b_c, c_c, acum_c, hs_c) + dsk * xs_c, rev)
            y_c = ydc if y_c is None else y_c + ydc
    return y_l, y_c


def token_mixer(pl, pc, cos, sin, conv_w, conv_b, a_log, dt_bias, d_skip, ssd_norm_g,
                lam, lam_init, subln_g, need_ctx):
    b, n = pl.shape[:2]
    nc = pc.shape[1]
    zl, xbcl, dtl, ql, kl, vl = jnp.split(pl, SPLITS, axis=-1)
    zc, xbcc, dtc, qc, kc, vc = jnp.split(pc, SPLITS, axis=-1)
    xbcl = jax.nn.silu(dwconv_centred(xbcl, conv_w, conv_b))
    xbcc = jax.nn.silu(dwconv_centred(xbcc, conv_w, conv_b))
    ys_l, ys_c = ssd_bidir(xbcl, dtl, xbcc, dtc, a_log, dt_bias, d_skip, need_ctx)
    ys_l = rmsnorm(ys_l.astype(pl.dtype) * jax.nn.silu(zl), ssd_norm_g)
    ql = apply_rope(ql.reshape(b, n, DA_HEADS, 2, DA_QK_DIM), cos, sin)
    kl = apply_rope(kl.reshape(b, n, DA_HEADS, 2, DA_QK_DIM), cos, sin)
    kc = kc.reshape(b, nc, DA_HEADS, 2, DA_QK_DIM)
    vc = vc.reshape(b, nc, DA_HEADS, DA_V_DIM)
    k_all = jnp.concatenate([kc, kl], axis=1)
    v_all = jnp.concatenate([vc, vl.reshape(b, n, DA_HEADS, DA_V_DIM)], axis=1)
    o_l = diff_attention_blocks(ql, k_all, v_all, lam).astype(pl.dtype)
    o_l = (rmsnorm(o_l, subln_g) * (1 - lam_init)).reshape(b, n, DA_INNER)
    yl = jnp.concatenate([ys_l, o_l], axis=-1)
    yc = None
    if need_ctx:
        o_c = diff_attention(qc.reshape(b, nc, DA_HEADS, 2, DA_QK_DIM), kc, vc, lam).astype(pc.dtype)
        o_c = (rmsnorm(o_c, subln_g) * (1 - lam_init)).reshape(b, nc, DA_INNER)
        ys_c = rmsnorm(ys_c.astype(pc.dtype) * jax.nn.silu(zc), ssd_norm_g)
        yc = jnp.concatenate([ys_c, o_c], axis=-1)
    return yl, yc


def conv_ffn(h, w_up, cw, cb, w_down):
    val, gate = jnp.split(h @ w_up, 2, axis=-1)
    gate = dwconv_centred(gate, cw, cb)
    return (jax.nn.silu(gate) * val) @ w_down


def setup_inputs(seed: int = 0) -> dict:
    key = jax.random.key(seed)
    ks = jax.random.split(key, 26)
    L = DEPTH

    def nrm(k, shape, scale):
        return jax.random.normal(k, shape, F32) * scale

    dt0 = jnp.exp(jax.random.uniform(ks[10], (L, 2, SSD_HEADS), F32, math.log(1e-3), math.log(1e-1)))
    return {
        'x': nrm(ks[0], (BATCH, SEQ, D_MODEL), 1.0),
        'c': nrm(ks[1], (BATCH, D_MODEL), 1.0),
        'ctx': nrm(ks[2], (BATCH, CTX_LEN, D_MODEL), 1.0),
        'c_ctx': nrm(ks[3], (D_MODEL,), 1.0),
        'w_mod': nrm(ks[4], (L, D_MODEL, 6 * D_MODEL), 0.5 * D_MODEL ** -0.5),
        'b_mod': nrm(ks[5], (L, 6 * D_MODEL), 0.01),
        'norm1_g': 1.0 + nrm(ks[6], (L, D_MODEL), 0.02),
        'w_in': nrm(ks[7], (L, D_MODEL, IN_WIDTH), D_MODEL ** -0.5),
        'conv_w': nrm(ks[8], (L, SSD_CONV, XBC_WIDTH), SSD_CONV ** -0.5),
        'conv_b': nrm(ks[9], (L, XBC_WIDTH), 0.02),
        'a_log': jnp.log(jax.random.uniform(ks[11], (L, 2, SSD_HEADS), F32, 1.0, 16.0)),
        'dt_bias': dt0 + jnp.log(-jnp.expm1(-dt0)),
        'd_skip': 1.0 + nrm(ks[12], (L, 2, SSD_HEADS), 0.1),
        'ssd_norm_g': 1.0 + nrm(ks[13], (L, SSD_INNER), 0.02),
        'lam_q1': nrm(ks[14], (L, DA_QK_DIM), 0.1),
        'lam_k1': nrm(ks[15], (L, DA_QK_DIM), 0.1),
        'lam_q2': nrm(ks[16], (L, DA_QK_DIM), 0.1),
        'lam_k2': nrm(ks[17], (L, DA_QK_DIM), 0.1),
        'subln_g': 1.0 + nrm(ks[18], (L, DA_V_DIM), 0.02),
        'w_out': nrm(ks[19], (L, MIX_WIDTH, D_MODEL), MIX_WIDTH ** -0.5),
        'norm2_g': 1.0 + nrm(ks[20], (L, D_MODEL), 0.02),
        'w_up': nrm(ks[21], (L, D_MODEL, 2 * D_FF), D_MODEL ** -0.5),
        'ffn_conv_w': nrm(ks[22], (L, FFN_CONV, D_FF), FFN_CONV ** -0.5),
        'ffn_conv_b': nrm(ks[23], (L, D_FF), 0.02),
        'w_down': nrm(ks[24], (L, D_FF, D_MODEL), D_FF ** -0.5),
        'final_g': 1.0 + nrm(ks[25], (D_MODEL,), 0.02),
    }


def reference(x, c, ctx, c_ctx, w_mod, b_mod, norm1_g, w_in, conv_w, conv_b, a_log, dt_bias,
              d_skip, ssd_norm_g, lam_q1, lam_k1, lam_q2, lam_k2, subln_g, w_out, norm2_g,
              w_up, ffn_conv_w, ffn_conv_b, w_down, final_g):
    n_lat = x.shape[1]
    rows = n_lat // GRID_W
    row = jnp.repeat(jnp.arange(rows), GRID_W)
    col = jnp.tile(jnp.arange(GRID_W), rows)
    cos, sin = axial_rope(row, col)
    xl, xc = x, ctx
    for i in range(DEPTH):
        need_ctx = i < DEPTH - 1
        lam_init = 0.8 - 0.6 * math.exp(-0.3 * i)
        lam = (jnp.exp(jnp.sum(lam_q1[i].astype(F32) * lam_k1[i].astype(F32)))
               - jnp.exp(jnp.sum(lam_q2[i].astype(F32) * lam_k2[i].astype(F32))) + lam_init)
        mod_l = jnp.split((jax.nn.silu(c) @ w_mod[i] + b_mod[i])[:, None, :], 6, axis=-1)
        mod_c = jnp.split((jax.nn.silu(c_ctx) @ w_mod[i] + b_mod[i])[None, None, :], 6, axis=-1)
        hl = modulate(rmsnorm(xl, norm1_g[i]), mod_l[0], mod_l[1])
        hc = modulate(rmsnorm(xc, norm1_g[i]), mod_c[0], mod_c[1])
        yl, yc = token_mixer(hl @ w_in[i], hc @ w_in[i], cos, sin, conv_w[i], conv_b[i], a_log[i],
                             dt_bias[i], d_skip[i], ssd_norm_g[i], lam, lam_init, subln_g[i], need_ctx)
        xl = xl + mod_l[2] * (yl @ w_out[i])
        xl = xl + mod_l[5] * conv_ffn(modulate(rmsnorm(xl, norm2_g[i]), mod_l[3], mod_l[4]),
                                      w_up[i], ffn_conv_w[i], ffn_conv_b[i], w_down[i])
        if need_ctx:
            xc = xc + mod_c[2] * (yc @ w_out[i])
            xc = xc + mod_c[5] * conv_ffn(modulate(rmsnorm(xc, norm2_g[i]), mod_c[3], mod_c[4]),
                                          w_up[i], ffn_conv_w[i], ffn_conv_b[i], w_down[i])
    return rmsnorm(xl, final_g)
```

```python
import functools
import math

import jax
import jax.numpy as jnp
from jax import lax
from jax.experimental import pallas as pl
from jax.experimental.pallas import tpu as pltpu

F32 = jnp.float32
BF16 = jnp.bfloat16

D_MODEL = 1024
GRID_W = 64
EPS = 1e-6

SSD_HEADS = 16
SSD_HEAD_DIM = 64
SSD_INNER = SSD_HEADS * SSD_HEAD_DIM
SSD_GROUPS = 2
SSD_HPG = SSD_HEADS // SSD_GROUPS
SSD_STATE = 128
SSD_CHUNK = 128
BC_WIDTH = SSD_GROUPS * SSD_STATE
XBC_WIDTH = SSD_INNER + 2 * BC_WIDTH

DA_HEADS = 4
DA_QK_DIM = 64
DA_V_DIM = 2 * DA_QK_DIM
DA_INNER = DA_HEADS * DA_V_DIM
ROPE_BASE = 10000.0
ROPE_AXIS_DIM = DA_QK_DIM // 2

D_FF = 2816

LANES = 128
BF16_SUBLANES = 16
VMEM_LIMIT = 56 * 1024 * 1024
NEG_BIG = -1e30

_NT = (((1,), (1,)), ((), ()))


def _dot(a, b):
    return jnp.dot(a, b, preferred_element_type=F32)


def _dot_nt(a, b):
    return lax.dot_general(a, b, _NT, preferred_element_type=F32)


def _split_bf16(v):
    hi = v.astype(BF16)
    lo = (v - hi.astype(F32)).astype(BF16)
    return hi, lo


def _silu(v):
    return v / (1.0 + jnp.exp(-v))


def _params(*sem):
    return pltpu.CompilerParams(dimension_semantics=sem, vmem_limit_bytes=VMEM_LIMIT)


def _mod_kernel(c_ref, w_ref, b_ref, o_ref):
    a_hi, a_lo = _split_bf16(_silu(c_ref[...]))
    w_hi, w_lo = _split_bf16(w_ref[...])
    o_ref[...] = _dot(a_hi, w_hi) + _dot(a_lo, w_hi) + _dot(a_hi, w_lo) + b_ref[...]


def _mod(cvec, w_mod, b_mod):
    rows, d = cvec.shape
    n = w_mod.shape[1]
    tn = 1536
    return pl.pallas_call(
        _mod_kernel,
        grid=(n // tn,),
        in_specs=[pl.BlockSpec((rows, d), lambda j: (0, 0)),
                  pl.BlockSpec((d, tn), lambda j: (0, j)),
                  pl.BlockSpec((1, tn), lambda j: (0, j))],
        out_specs=pl.BlockSpec((rows, tn), lambda j: (0, j)),
        out_shape=jax.ShapeDtypeStruct((rows, n), F32),
        compiler_params=_params("arbitrary"),
        name="mod",
    )(cvec, w_mod, b_mod.reshape(1, n))


def _rmsnorm_mod(x, g, shift, scale):
    y = x * lax.rsqrt(jnp.mean(x * x, axis=-1, keepdims=True) + EPS) * g
    return y * (1.0 + scale) + shift


def _rope(t, cos, sin_a, sin_b):
    return (t * cos + pltpu.roll(t, LANES - ROPE_AXIS_DIM // 2, 1) * sin_a
            + pltpu.roll(t, ROPE_AXIS_DIM // 2, 1) * sin_b)


def _inproj_kernel(x_ref, g_ref, shift_ref, scale_ref, cos_ref, sa_ref, sb_ref,
                   wz_ref, wxbc_ref, wdt_ref, wq_ref, wk_ref, wv_ref,
                   z_ref, xbc_ref, dt_ref, q_ref, k_ref, v_ref, *, q_scale):
    h = _rmsnorm_mod(x_ref[0], g_ref[...], shift_ref[0], scale_ref[0]).astype(BF16)
    z_ref[0] = _dot(h, wz_ref[...]).astype(BF16)
    xbc_ref[0] = _dot(h, wxbc_ref[...]).astype(BF16)
    dt = _dot(h, wdt_ref[...])
    dt_ref[0, 0] = dt[:, :LANES]
    dt_ref[0, 1] = dt[:, LANES:]
    v_ref[0] = _dot(h, wv_ref[...]).astype(BF16)
    cos, sa, sb = cos_ref[...], sa_ref[...], sb_ref[...]
    lane = lax.broadcasted_iota(jnp.int32, cos.shape, 1)
    first_map = lane < DA_QK_DIM
    q = _dot(h, wq_ref[...])
    k = _dot(h, wk_ref[...])
    for hd in range(DA_HEADS):
        sl = slice(hd * LANES, (hd + 1) * LANES)
        qh = _rope(q[:, sl], cos, sa, sb) * q_scale
        q_ref[0, hd, 0] = jnp.where(first_map, qh, 0.0).astype(BF16)
        q_ref[0, hd, 1] = jnp.where(first_map, 0.0, qh).astype(BF16)
        k_ref[0, :, sl] = _rope(k[:, sl], cos, sa, sb).astype(BF16)


def _inproj(x, g, shift, scale, rope, w, q_scale):
    b, s, d = x.shape
    tm = min(512, s)
    cos, sa, sb = rope
    full = lambda a: pl.BlockSpec(a.shape, lambda bi, i: (0,) * a.ndim)
    row = lambda width: pl.BlockSpec((1, tm, width), lambda bi, i: (bi, i, 0))
    vec = pl.BlockSpec((1, 1, d), lambda bi, i: (bi, 0, 0))
    tab = pl.BlockSpec((tm, LANES), lambda bi, i: (i, 0))
    return pl.pallas_call(
        functools.partial(_inproj_kernel, q_scale=q_scale),
        grid=(b, s // tm),
        in_specs=[row(d), full(g), vec, vec, tab, tab, tab] + [full(a) for a in w],
        out_specs=[row(SSD_INNER), row(XBC_WIDTH),
                   pl.BlockSpec((1, 2, tm, LANES), lambda bi, i: (bi, 0, i, 0)),
                   pl.BlockSpec((1, DA_HEADS, 2, tm, LANES), lambda bi, i: (bi, 0, 0, i, 0)),
                   row(DA_INNER), row(DA_INNER)],
        out_shape=[jax.ShapeDtypeStruct((b, s, SSD_INNER), BF16),
                   jax.ShapeDtypeStruct((b, s, XBC_WIDTH), BF16),
                   jax.ShapeDtypeStruct((b, 2, s, LANES), F32),
                   jax.ShapeDtypeStruct((b, DA_HEADS, 2, s, LANES), BF16),
                   jax.ShapeDtypeStruct((b, s, DA_INNER), BF16),
                   jax.ShapeDtypeStruct((b, s, DA_INNER), BF16)],
        compiler_params=_params("parallel", "parallel"),
        name="inproj",
    )(x, g, shift, scale, cos, sa, sb, *w)


def _dwconv_kernel(x_ref, prev_ref, next_ref, w_ref, b_ref, o_ref, buf_ref):
    i = pl.program_id(1)
    last = pl.num_programs(1) - 1
    t = x_ref.shape[1]
    hl = prev_ref.shape[1]
    buf_ref[0:hl, :] = jnp.where(i > 0, prev_ref[0].astype(F32), 0.0)
    buf_ref[hl:hl + t, :] = x_ref[0].astype(F32)
    buf_ref[hl + t:, :] = jnp.where(i < last, next_ref[0].astype(F32), 0.0)
    w = w_ref[...]
    acc = (b_ref[...] + buf_ref[hl - 1:hl - 1 + t, :] * w[0:1]
           + buf_ref[hl:hl + t, :] * w[1:2] + buf_ref[hl + 1:hl + 1 + t, :] * w[2:3])
    o_ref[0] = _silu(acc).astype(o_ref.dtype)


def _dwconv_silu(x, w, bias):
    b, s, c = x.shape
    t = min(512, s)
    hl = BF16_SUBLANES
    nh = s // hl
    per = t // hl
    return pl.pallas_call(
        _dwconv_kernel,
        grid=(b, s // t),
        in_specs=[pl.BlockSpec((1, t, c), lambda bi, i: (bi, i, 0)),
                  pl.BlockSpec((1, hl, c), lambda bi, i: (bi, jnp.maximum(i * per - 1, 0), 0)),
                  pl.BlockSpec((1, hl, c), lambda bi, i: (bi, jnp.minimum((i + 1) * per, nh - 1), 0)),
                  pl.BlockSpec((8, c), lambda bi, i: (0, 0)),
                  pl.BlockSpec((1, c), lambda bi, i: (0, 0))],
        out_specs=pl.BlockSpec((1, t, c), lambda bi, i: (bi, i, 0)),
        out_shape=jax.ShapeDtypeStruct((b, s, c), BF16),
        scratch_shapes=[pltpu.VMEM((t + 2 * hl, c), F32)],
        compiler_params=_params("parallel", "parallel"),
        name="dwconv_silu",
    )(x, x, x, jnp.pad(w.astype(F32), ((0, 8 - w.shape[0]), (0, 0))), bias.astype(F32).reshape(1, c))


def _ssd_kernel(xbc_ref, dt_ref, h0_ref, tri_ref, eh_ref, a_ref, dtb_ref, dsk_ref, *rest, emit_y):
    if emit_y:
        y_ref, ht_ref, state_ref = rest
    else:
        ht_ref, state_ref = rest
    d = pl.program_id(1)
    c = pl.program_id(2)
    nc = pl.num_programs(2)
    L = SSD_CHUNK
    half = SSD_INNER // SSD_GROUPS

    @pl.when(c == 0)
    def _():
        state_ref[...] = h0_ref[0, 0]

    xbc_b = xbc_ref[0]
    xs_b = xbc_b[:, :SSD_INNER]
    bm_b = xbc_b[:, SSD_INNER:SSD_INNER + BC_WIDTH]
    cm_b = xbc_b[:, SSD_INNER + BC_WIDTH:]
    xs = xs_b.astype(F32)

    raw = dt_ref[0, 0] + dtb_ref[0]
    dt = jnp.maximum(raw, 0.0) + jnp.log1p(jnp.exp(-jnp.abs(raw)))
    da = dt * a_ref[0]
    tri = tri_ref[0]
    da_hi, da_lo = _split_bf16(da)
    acum = _dot(tri, da_hi) + _dot(tri, da_lo)
    total = jnp.sum(da, axis=0, keepdims=True)
    dte = jnp.exp(total - acum) * dt
    eac = jnp.exp(acum)
    cdec = jnp.broadcast_to(jnp.exp(total), (8, LANES))

    eh = eh_ref[...]
    dte_x = _dot(dte.astype(BF16), eh)
    c_hi, c_lo = _split_bf16(cdec)
    cdec_x = (_dot(c_hi, eh) + _dot(c_lo, eh))[0:1]

    st_old = state_ref[...]
    xd = (xs * dte_x).astype(BF16)
    st_new = []
    for g in range(SSD_GROUPS):
        bg_t = bm_b[:, g * SSD_STATE:(g + 1) * SSD_STATE].astype(F32).T.astype(BF16)
        st_new.append(_dot(bg_t, xd[:, g * half:(g + 1) * half]))
    state_ref[...] = cdec_x * st_old + jnp.concatenate(st_new, axis=1)

    @pl.when(c == nc - 1)
    def _():
        ht_ref[0, 0] = state_ref[...]

    if emit_y:
        eac_x = _dot(eac.astype(BF16), eh)
        st_b = st_old.astype(BF16)
        acum_t = acum.T
        dt_t = dt.T
        row = lax.broadcasted_iota(jnp.int32, (L, L), 0)
        col = lax.broadcasted_iota(jnp.int32, (L, L), 1)
        mask = (row - col) * (1 - 2 * d) >= 0
        lane = lax.broadcasted_iota(jnp.int32, (L, LANES), 1)
        lo_half = lane < SSD_HEAD_DIM
        y_diag, y_off = [], []
        for g in range(SSD_GROUPS):
            cg = cm_b[:, g * SSD_STATE:(g + 1) * SSD_STATE]
            bg = bm_b[:, g * SSD_STATE:(g + 1) * SSD_STATE]
            scores = _dot_nt(cg, bg)
            y_off.append(_dot(cg, st_b[:, g * half:(g + 1) * half]))
            for j in range(SSD_HPG // 2):
                ws = []
                for hh in (g * SSD_HPG + 2 * j, g * SSD_HPG + 2 * j + 1):
                    seg = acum[:, hh:hh + 1] - acum_t[hh:hh + 1, :]
                    dec = jnp.exp(jnp.where(mask, seg, NEG_BIG))
                    ws.append(scores * dec * dt_t[hh:hh + 1, :])
                wp = jnp.concatenate(ws, axis=1).astype(BF16)
                p = g * (SSD_HPG // 2) + j
                slab = xs_b[:, p * LANES:(p + 1) * LANES]
                zero = jnp.zeros_like(slab)
                bd = jnp.concatenate([jnp.where(lo_half, slab, zero),
                                      jnp.where(lo_half, zero, slab)], axis=0)
                y_diag.append(_dot(wp, bd))
        y = (jnp.concatenate(y_diag, axis=1) + jnp.concatenate(y_off, axis=1) * eac_x
             + dsk_ref[0] * xs)
        y_ref[0, 0] = y.astype(BF16)


def _ssd_scan(xbc, dt, h0, consts, emit_y):
    b, s, _ = xbc.shape
    nc = s // SSD_CHUNK
    L = SSD_CHUNK
    tri, eh, a, dtb, dsk = consts
    chunk = lambda c, d: c + d * (nc - 1 - 2 * c)
    perdir = lambda width: pl.BlockSpec((1, 1, width), lambda bi, d, c: (d, 0, 0))
    state_spec = pl.BlockSpec((1, 1, SSD_STATE, SSD_INNER), lambda bi, d, c: (bi, d, 0, 0))
    out_specs = [state_spec]
    out_shape = [jax.ShapeDtypeStruct((b, 2, SSD_STATE, SSD_INNER), F32)]
    if emit_y:
        out_specs.insert(0, pl.BlockSpec((1, 1, L, SSD_INNER), lambda bi, d, c: (bi, d, chunk(c, d), 0)))
        out_shape.insert(0, jax.ShapeDtypeStruct((b, 2, s, SSD_INNER), BF16))
    return pl.pallas_call(
        functools.partial(_ssd_kernel, emit_y=emit_y),
        grid=(b, 2, nc),
        in_specs=[pl.BlockSpec((1, L, XBC_WIDTH), lambda bi, d, c: (bi, chunk(c, d), 0)),
                  pl.BlockSpec((1, 1, L, LANES), lambda bi, d, c: (bi, d, chunk(c, d), 0)),
                  state_spec,
                  pl.BlockSpec((1, L, L), lambda bi, d, c: (d, 0, 0)),
                  pl.BlockSpec(eh.shape, lambda bi, d, c: (0, 0)),
                  perdir(LANES), perdir(LANES), perdir(SSD_INNER)],
        out_specs=out_specs,
        out_shape=out_shape,
        scratch_shapes=[pltpu.VMEM((SSD_STATE, SSD_INNER), F32)],
        compiler_params=_params("parallel", "parallel", "arbitrary"),
        name="ssd_scan" if emit_y else "ssd_state",
    )(xbc, dt, h0, tri, eh, a, dtb, dsk)


def _attn_kernel(q_ref, k_ref, v_ref, lamv_ref, g_ref, o_ref, m_ref, l_ref, acc_ref, *, tk, lam_init):
    tq = q_ref.shape[3]
    nk = k_ref.shape[1] // tk
    q = q_ref[0, 0].reshape(2 * tq, LANES)
    m_ref[...] = jnp.full(m_ref.shape, NEG_BIG, F32)
    l_ref[...] = jnp.zeros(l_ref.shape, F32)
    acc_ref[...] = jnp.zeros(acc_ref.shape, F32)

    def body(j, carry):
        off = pl.multiple_of(j * tk, tk)
        k = k_ref[0, pl.ds(off, tk), :]
        v = v_ref[0, pl.ds(off, tk), :]
        s = _dot_nt(q, k)
        m_old = m_ref[...]
        m_new = jnp.maximum(m_old, jnp.max(s, axis=-1, keepdims=True))
        alpha = jnp.exp2(m_old - m_new)
        p = jnp.exp2(s - m_new)
        l_ref[...] = alpha * l_ref[...] + jnp.sum(p, axis=-1, keepdims=True)
        m_ref[...] = m_new
        acc_ref[...] = alpha * acc_ref[...] + _dot(p.astype(BF16), v)
        return carry

    lax.fori_loop(0, nk, body, 0)

    lv = lamv_ref[...]
    lam = (jnp.exp(jnp.sum(lv[0:1] * lv[1:2], keepdims=True))
           - jnp.exp(jnp.sum(lv[2:3] * lv[3:4], keepdims=True)) + lam_init)
    o = acc_ref[...] / l_ref[...]
    o = o[:tq] - lam * o[tq:]
    o = o * lax.rsqrt(jnp.mean(o * o, axis=-1, keepdims=True) + EPS) * g_ref[...]
    o_ref[0] = (o * (1.0 - lam_init)).astype(BF16)


def _attention(q, k, v, lamv, subln_g, lam_init):
    b, nh, _, s, _ = q.shape
    n_keys = k.shape[1]
    tq = min(512, s)
    tk = 768 if n_keys % 768 == 0 else 256
    return pl.pallas_call(
        functools.partial(_attn_kernel, tk=tk, lam_init=lam_init),
        grid=(b, nh, s // tq),
        in_specs=[pl.BlockSpec((1, 1, 2, tq, LANES), lambda bi, h, i: (bi, h, 0, i, 0)),
                  pl.BlockSpec((1, n_keys, LANES), lambda bi, h, i: (bi, 0, h)),
                  pl.BlockSpec((1, n_keys, LANES), lambda bi, h, i: (bi, 0, h)),
                  pl.BlockSpec((8, LANES), lambda bi, h, i: (0, 0)),
                  pl.BlockSpec((1, LANES), lambda bi, h, i: (0, 0))],
        out_specs=pl.BlockSpec((1, tq, LANES), lambda bi, h, i: (bi, i, h)),
        out_shape=jax.ShapeDtypeStruct((b, s, DA_INNER), BF16),
        scratch_shapes=[pltpu.VMEM((2 * tq, 1), F32), pltpu.VMEM((2 * tq, 1), F32),
                        pltpu.VMEM((2 * tq, LANES), F32)],
        compiler_params=_params("parallel", "parallel", "parallel"),
        name="diff_attn",
    )(q, k, v, lamv, subln_g)


def _outproj_kernel(yf_ref, yb_ref, z_ref, o_ref, x_ref, gs_ref, wa_ref, wb_ref, gate_ref,
                    g2_ref, shift_ref, scale_ref, xl_ref, h2_ref):
    z = z_ref[0].astype(F32)
    t = (yf_ref[0, 0].astype(F32) + yb_ref[0, 0].astype(F32)) * _silu(z)
    ys = t * lax.rsqrt(jnp.mean(t * t, axis=-1, keepdims=True) + EPS) * gs_ref[...]
    mix = _dot(ys.astype(BF16), wa_ref[...]) + _dot(o_ref[0], wb_ref[...])
    xl = x_ref[0] + gate_ref[0] * mix
    xl_ref[0] = xl
    h2_ref[0] = _rmsnorm_mod(xl, g2_ref[...], shift_ref[0], scale_ref[0]).astype(BF16)


def _outproj(y, z, o, x, gs, wa, wb, gate, g2, shift, scale):
    b, s, d = x.shape
    tm = min(512, s)
    full = lambda a: pl.BlockSpec(a.shape, lambda bi, i: (0,) * a.ndim)
    row = lambda width: pl.BlockSpec((1, tm, width), lambda bi, i: (bi, i, 0))
    vec = pl.BlockSpec((1, 1, d), lambda bi, i: (bi, 0, 0))
    ydir = lambda dd: pl.BlockSpec((1, 1, tm, SSD_INNER), lambda bi, i: (bi, dd, i, 0))
    return pl.pallas_call(
        _outproj_kernel,
        grid=(b, s // tm),
        in_specs=[ydir(0), ydir(1), row(SSD_INNER), row(DA_INNER), row(d), full(gs), full(wa), full(wb),
                  vec, full(g2), vec, vec],
        out_specs=[row(d), row(d)],
        out_shape=[jax.ShapeDtypeStruct((b, s, d), F32), jax.ShapeDtypeStruct((b, s, d), BF16)],
        compiler_params=_params("parallel", "parallel"),
        name="outproj",
    )(y, y, z, o, x, gs, wa, wb, gate, g2, shift, scale)


def _ffn_up_kernel(h_ref, wv_ref, wg_ref, val_ref, gate_ref):
    h = h_ref[0]
    val_ref[0] = _dot(h, wv_ref[...]).astype(BF16)
    gate_ref[0] = _dot(h, wg_ref[...]).astype(BF16)


def _ffn_up(h, wv, wg):
    b, s, d = h.shape
    tm = min(512, s)
    full = lambda a: pl.BlockSpec(a.shape, lambda bi, i: (0,) * a.ndim)
    row = lambda width: pl.BlockSpec((1, tm, width), lambda bi, i: (bi, i, 0))
    return pl.pallas_call(
        _ffn_up_kernel,
        grid=(b, s // tm),
        in_specs=[row(d), full(wv), full(wg)],
        out_specs=[row(D_FF), row(D_FF)],
        out_shape=[jax.ShapeDtypeStruct((b, s, D_FF), BF16)] * 2,
        compiler_params=_params("parallel", "parallel"),
        name="ffn_up",
    )(h, wv, wg)


def _ffn_down_kernel(val_ref, cg_ref, xl_ref, w_ref, gate_ref, gf_ref, o_ref):
    u = (val_ref[0].astype(F32) * cg_ref[0].astype(F32)).astype(BF16)
    x = xl_ref[0] + gate_ref[0] * _dot(u, w_ref[...])
    o_ref[0] = x * lax.rsqrt(jnp.mean(x * x, axis=-1, keepdims=True) + EPS) * gf_ref[...]


def _ffn_down(val, cg, xl, w, gate, gf):
    b, s, d = xl.shape
    tm = min(512, s)
    full = lambda a: pl.BlockSpec(a.shape, lambda bi, i: (0,) * a.ndim)
    row = lambda width: pl.BlockSpec((1, tm, width), lambda bi, i: (bi, i, 0))
    vec = pl.BlockSpec((1, 1, d), lambda bi, i: (bi, 0, 0))
    return pl.pallas_call(
        _ffn_down_kernel,
        grid=(b, s // tm),
        in_specs=[row(D_FF), row(D_FF), row(d), full(w), vec, full(gf)],
        out_specs=row(d),
        out_shape=jax.ShapeDtypeStruct((b, s, d), F32),
        compiler_params=_params("parallel", "parallel"),
        name="ffn_down",
    )(val, cg, xl, w, gate, gf)


def _rope_tables(n):
    inv = jnp.power(ROPE_BASE, -jnp.arange(0, ROPE_AXIS_DIM, 2, dtype=F32) / ROPE_AXIS_DIM)
    t = jnp.arange(n)
    ar = (t // GRID_W).astype(F32)[:, None] * inv
    ac = (t % GRID_W).astype(F32)[:, None] * inv
    ang = jnp.concatenate([ar, ar, ac, ac] * 2, axis=-1)
    first = (jnp.arange(LANES) % ROPE_AXIS_DIM) < ROPE_AXIS_DIM // 2
    sin = jnp.sin(ang)
    return jnp.cos(ang), jnp.where(first, -sin, 0.0), jnp.where(first, 0.0, sin)


def kernel(x, c, ctx, c_ctx, w_mod, b_mod, norm1_g, w_in, conv_w, conv_b, a_log, dt_bias, d_skip,
           ssd_norm_g, lam_q1, lam_k1, lam_q2, lam_k2, subln_g, w_out, norm2_g, w_up, ffn_conv_w,
           ffn_conv_b, w_down, final_g):
    b, n, d = x.shape
    nctx = ctx.shape[1]
    lam_init = 0.8 - 0.6 * math.exp(-0.3 * 0)
    q_scale = DA_QK_DIM ** -0.5 * math.log2(math.e)

    cvec = jnp.concatenate([c, c_ctx[None], jnp.zeros((8 - b - 1, d), F32)], axis=0)
    mod = _mod(cvec, w_mod[0], b_mod[0])
    mod_l = [mod[:b, i * d:(i + 1) * d].reshape(b, 1, d) for i in range(6)]
    mod_c = [jnp.broadcast_to(mod[b, i * d:(i + 1) * d].reshape(1, 1, d), (b, 1, d)) for i in range(2)]

    wi = w_in[0].astype(BF16)
    o_x, o_dt, o_q = SSD_INNER, SSD_INNER + XBC_WIDTH, SSD_INNER + XBC_WIDTH + 2 * SSD_HEADS
    pad_dt = lambda a: jnp.pad(a, ((0, 0), (0, LANES - SSD_HEADS)))
    w_parts = (wi[:, :o_x], wi[:, o_x:o_dt],
               jnp.concatenate([pad_dt(wi[:, o_dt:o_dt + SSD_HEADS]), pad_dt(wi[:, o_dt + SSD_HEADS:o_q])], axis=1),
               wi[:, o_q:o_q + DA_INNER], wi[:, o_q + DA_INNER:o_q + 2 * DA_INNER], wi[:, o_q + 2 * DA_INNER:])
    g1 = norm1_g[0].reshape(1, d)
    rope_l = _rope_tables(n)
    rope_c = (jnp.ones((nctx, LANES), F32), jnp.zeros((nctx, LANES), F32), jnp.zeros((nctx, LANES), F32))
    z_l, xbc_l, dt_l, q_l, k_l, v_l = _inproj(x, g1, mod_l[0], mod_l[1], rope_l, w_parts, q_scale)
    _, xbc_c, dt_c, _, k_c, v_c = _inproj(ctx, g1, mod_c[0], mod_c[1], rope_c, w_parts, q_scale)

    xbc_l = _dwconv_silu(xbc_l, conv_w[0], conv_b[0])
    xbc_c = _dwconv_silu(xbc_c, conv_w[0], conv_b[0])
    idx = jnp.arange(SSD_CHUNK)
    tri = jnp.stack([idx[:, None] >= idx[None, :], idx[:, None] <= idx[None, :]]).astype(BF16)
    eh = (jnp.arange(LANES)[:, None] == (jnp.arange(SSD_INNER) // SSD_HEAD_DIM)[None, :]).astype(BF16)
    lanes16 = lambda a: jnp.pad(a.astype(F32), ((0, 0), (0, LANES - SSD_HEADS))).reshape(2, 1, LANES)
    consts = (tri, eh, lanes16(-jnp.exp(a_log[0].astype(F32))), lanes16(dt_bias[0]),
              jnp.repeat(d_skip[0].astype(F32), SSD_HEAD_DIM, axis=1).reshape(2, 1, SSD_INNER))
    h_zero = jnp.zeros((b, 2, SSD_STATE, SSD_INNER), F32)
    (h_ctx,) = _ssd_scan(xbc_c, dt_c, h_zero, consts, emit_y=False)
    y_l, _ = _ssd_scan(xbc_l, dt_l, h_ctx, consts, emit_y=True)

    k_all = jnp.concatenate([k_l, k_c], axis=1)
    v_all = jnp.concatenate([v_l, v_c], axis=1)
    lamv = jnp.zeros((8, LANES), F32).at[:4, :DA_QK_DIM].set(
        jnp.stack([lam_q1[0], lam_k1[0], lam_q2[0], lam_k2[0]]).astype(F32))
    o_l = _attention(q_l, k_all, v_all, lamv, subln_g[0].reshape(1, DA_V_DIM).astype(F32), lam_init)

    wo = w_out[0].astype(BF16)
    xl, h2 = _outproj(y_l, z_l, o_l, x, ssd_norm_g[0].reshape(1, SSD_INNER), wo[:SSD_INNER], wo[SSD_INNER:],
                      mod_l[2], norm2_g[0].reshape(1, d), mod_l[3], mod_l[4])
    wu = w_up[0].astype(BF16)
    val, gate = _ffn_up(h2, wu[:, :D_FF], wu[:, D_FF:])
    cg = _dwconv_silu(gate, ffn_conv_w[0], ffn_conv_b[0])
    return _ffn_down(val, cg, xl, w_down[0].astype(BF16), mod_l[5], final_g.reshape(1, d))
```

```python
import functools
import math

import jax
import jax.numpy as jnp
from jax import lax
from jax.experimental import pallas as pl
from jax.experimental.pallas import tpu as pltpu

F32 = jnp.float32
BF16 = jnp.bfloat16

D_MODEL = 1024
GRID_W = 64
EPS = 1e-6

SSD_HEADS = 16
SSD_HEAD_DIM = 64
SSD_INNER = SSD_HEADS * SSD_HEAD_DIM
SSD_GROUPS = 2
SSD_HPG = SSD_HEADS // SSD_GROUPS
SSD_STATE = 128
SSD_CHUNK = 128
BC_WIDTH = SSD_GROUPS * SSD_STATE
XBC_WIDTH = SSD_INNER + 2 * BC_WIDTH

DA_HEADS = 4
DA_QK_DIM = 64
DA_V_DIM = 2 * DA_QK_DIM
DA_INNER = DA_HEADS * DA_V_DIM
ROPE_BASE = 10000.0
ROPE_AXIS_DIM = DA_QK_DIM // 2

D_FF = 2816

LANES = 128
BF16_SUBLANES = 16
VMEM_LIMIT = 56 * 1024 * 1024
NEG_BIG = -1e30

_NT = (((1,), (1,)), ((), ()))


def _dot(a, b):
    return jnp.dot(a, b, preferred_element_type=F32)


def _dot_nt(a, b):
    return lax.dot_general(a, b, _NT, preferred_element_type=F32)


def _split_bf16(v):
    hi = v.astype(BF16)
    lo = (v - hi.astype(F32)).astype(BF16)
    return hi, lo


def _silu(v):
    return v / (1.0 + jnp.exp(-v))


def _params(*sem):
    return pltpu.CompilerParams(dimension_semantics=sem, vmem_limit_bytes=VMEM_LIMIT)


def _mod_kernel(c_ref, w_ref, b_ref, o_ref):
    a_hi, a_lo = _split_bf16(_silu(c_ref[...]))
    w_hi, w_lo = _split_bf16(w_ref[...])
    o_ref[...] = _dot(a_hi, w_hi) + _dot(a_lo, w_hi) + _dot(a_hi, w_lo) + b_ref[...]


def _mod(cvec, w_mod, b_mod):
    rows, d = cvec.shape
    n = w_mod.shape[1]
    tn = 1536
    return pl.pallas_call(
        _mod_kernel,
        grid=(n // tn,),
        in_specs=[pl.BlockSpec((rows, d), lambda j: (0, 0)),
                  pl.BlockSpec((d, tn), lambda j: (0, j)),
                  pl.BlockSpec((1, tn), lambda j: (0, j))],
        out_specs=pl.BlockSpec((rows, tn), lambda j: (0, j)),
        out_shape=jax.ShapeDtypeStruct((rows, n), F32),
        compiler_params=_params("arbitrary"),
        name="mod",
    )(cvec, w_mod, b_mod.reshape(1, n))


def _rmsnorm_mod(x, g, shift, scale):
    y = x * lax.rsqrt(jnp.mean(x * x, axis=-1, keepdims=True) + EPS) * g
    return y * (1.0 + scale) + shift


def _rope(t, cos, sin_a, sin_b):
    return (t * cos + pltpu.roll(t, LANES - ROPE_AXIS_DIM // 2, 1) * sin_a
            + pltpu.roll(t, ROPE_AXIS_DIM // 2, 1) * sin_b)


def _inproj_kernel(x_ref, g_ref, shift_ref, scale_ref, cos_ref, sa_ref, sb_ref,
                   wz_ref, wxbc_ref, wdt_ref, wq_ref, wk_ref, wv_ref,
                   z_ref, xbc_ref, dt_ref, q_ref, k_ref, v_ref, *, q_scale):
    h = _rmsnorm_mod(x_ref[0], g_ref[...], shift_ref[0], scale_ref[0]).astype(BF16)
    z_ref[0] = _dot(h, wz_ref[...]).astype(BF16)
    xbc_ref[0] = _dot(h, wxbc_ref[...]).astype(BF16)
    dt = _dot(h, wdt_ref[...])
    dt_ref[0, 0] = dt[:, :LANES]
    dt_ref[0, 1] = dt[:, LANES:]
    v_ref[0] = _dot(h, wv_ref[...]).astype(BF16)
    cos, sa, sb = cos_ref[...], sa_ref[...], sb_ref[...]
    lane = lax.broadcasted_iota(jnp.int32, cos.shape, 1)
    first_map = lane < DA_QK_DIM
    q = _dot(h, wq_ref[...])
    k = _dot(h, wk_ref[...])
    for hd in range(DA_HEADS):
        sl = slice(hd * LANES, (hd + 1) * LANES)
        qh = _rope(q[:, sl], cos, sa, sb) * q_scale
        q_ref[0, hd, 0] = jnp.where(first_map, qh, 0.0).astype(BF16)
        q_ref[0, hd, 1] = jnp.where(first_map, 0.0, qh).astype(BF16)
        k_ref[0, :, sl] = _rope(k[:, sl], cos, sa, sb).astype(BF16)


def _inproj(x, g, shift, scale, rope, w, q_scale):
    b, s, d = x.shape
    tm = min(512, s)
    cos, sa, sb = rope
    full = lambda a: pl.BlockSpec(a.shape, lambda bi, i: (0,) * a.ndim)
    row = lambda width: pl.BlockSpec((1, tm, width), lambda bi, i: (bi, i, 0))
    vec = pl.BlockSpec((1, 1, d), lambda bi, i: (bi, 0, 0))
    tab = pl.BlockSpec((tm, LANES), lambda bi, i: (i, 0))
    return pl.pallas_call(
        functools.partial(_inproj_kernel, q_scale=q_scale),
        grid=(b, s // tm),
        in_specs=[row(d), full(g), vec, vec, tab, tab, tab] + [full(a) for a in w],
        out_specs=[row(SSD_INNER), row(XBC_WIDTH),
                   pl.BlockSpec((1, 2, tm, LANES), lambda bi, i: (bi, 0, i, 0)),
                   pl.BlockSpec((1, DA_HEADS, 2, tm, LANES), lambda bi, i: (bi, 0, 0, i, 0)),
                   row(DA_INNER), row(DA_INNER)],
        out_shape=[jax.ShapeDtypeStruct((b, s, SSD_INNER), BF16),
                   jax.ShapeDtypeStruct((b, s, XBC_WIDTH), BF16),
                   jax.ShapeDtypeStruct((b, 2, s, LANES), F32),
                   jax.ShapeDtypeStruct((b, DA_HEADS, 2, s, LANES), BF16),
                   jax.ShapeDtypeStruct((b, s, DA_INNER), BF16),
                   jax.ShapeDtypeStruct((b, s, DA_INNER), BF16)],
        compiler_params=_params("parallel", "parallel"),
        name="inproj",
    )(x, g, shift, scale, cos, sa, sb, *w)


def _dwconv_kernel(x_ref, prev_ref, next_ref, w_ref, b_ref, o_ref, buf_ref):
    i = pl.program_id(1)
    last = pl.num_programs(1) - 1
    t = x_ref.shape[1]
    hl = prev_ref.shape[1]
    buf_ref[0:hl, :] = jnp.where(i > 0, prev_ref[0].astype(F32), 0.0)
    buf_ref[hl:hl + t, :] = x_ref[0].astype(F32)
    buf_ref[hl + t:, :] = jnp.where(i < last, next_ref[0].astype(F32), 0.0)
    w = w_ref[...]
    acc = (b_ref[...] + buf_ref[hl - 1:hl - 1 + t, :] * w[0:1]
           + buf_ref[hl:hl + t, :] * w[1:2] + buf_ref[hl + 1:hl + 1 + t, :] * w[2:3])
    o_ref[0] = _silu(acc).astype(o_ref.dtype)


def _dwconv_silu(x, w, bias):
    b, s, c = x.shape
    t = min(512, s)
    hl = BF16_SUBLANES
    nh = s // hl
    per = t // hl
    return pl.pallas_call(
        _dwconv_kernel,
        grid=(b, s // t),
        in_specs=[pl.BlockSpec((1, t, c), lambda bi, i: (bi, i, 0)),
                  pl.BlockSpec((1, hl, c), lambda bi, i: (bi, jnp.maximum(i * per - 1, 0), 0)),
                  pl.BlockSpec((1, hl, c), lambda bi, i: (bi, jnp.minimum((i + 1) * per, nh - 1), 0)),
                  pl.BlockSpec((8, c), lambda bi, i: (0, 0)),
                  pl.BlockSpec((1, c), lambda bi, i: (0, 0))],
        out_specs=pl.BlockSpec((1, t, c), lambda bi, i: (bi, i, 0)),
        out_shape=jax.ShapeDtypeStruct((b, s, c), BF16),
        scratch_shapes=[pltpu.VMEM((t + 2 * hl, c), F32)],
        compiler_params=_params("parallel", "parallel"),
        name="dwconv_silu",
    )(x, x, x, jnp.pad(w.astype(F32), ((0, 8 - w.shape[0]), (0, 0))), bias.astype(F32).reshape(1, c))


def _ssd_kernel(xbc_ref, dt_ref, h0_ref, tri_ref, eh_ref, a_ref, dtb_ref, dsk_ref, *rest, emit_y):
    if emit_y:
        y_ref, ht_ref, state_ref = rest
    else:
        ht_ref, state_ref = rest
    d = pl.program_id(1)
    c = pl.program_id(2)
    nc = pl.num_programs(2)
    L = SSD_CHUNK
    half = SSD_INNER // SSD_GROUPS

    @pl.when(c == 0)
    def _():
        state_ref[...] = h0_ref[0, 0]

    xbc_b = xbc_ref[0]
    xs_b = xbc_b[:, :SSD_INNER]
    bm_b = xbc_b[:, SSD_INNER:SSD_INNER + BC_WIDTH]
    cm_b = xbc_b[:, SSD_INNER + BC_WIDTH:]
    xs = xs_b.astype(F32)

    raw = dt_ref[0, 0] + dtb_ref[0]
    dt = jnp.maximum(raw, 0.0) + jnp.log1p(jnp.exp(-jnp.abs(raw)))
    da = dt * a_ref[0]
    tri = tri_ref[0]
    da_hi, da_lo = _split_bf16(da)
    acum = _dot(tri, da_hi) + _dot(tri, da_lo)
    total = jnp.sum(da, axis=0, keepdims=True)
    dte = jnp.exp(total - acum) * dt
    eac = jnp.exp(acum)
    cdec = jnp.broadcast_to(jnp.exp(total), (8, LANES))

    eh = eh_ref[...]
    dte_x = _dot(dte.astype(BF16), eh)
    c_hi, c_lo = _split_bf16(cdec)
    cdec_x = (_dot(c_hi, eh) + _dot(c_lo, eh))[0:1]

    st_old = state_ref[...]
    xd = (xs * dte_x).astype(BF16)
    st_new = []
    for g in range(SSD_GROUPS):
        bg_t = bm_b[:, g * SSD_STATE:(g + 1) * SSD_STATE].astype(F32).T.astype(BF16)
        st_new.append(_dot(bg_t, xd[:, g * half:(g + 1) * half]))
    state_ref[...] = cdec_x * st_old + jnp.concatenate(st_new, axis=1)

    @pl.when(c == nc - 1)
    def _():
        ht_ref[0, 0] = state_ref[...]

    if emit_y:
        eac_x = _dot(eac.astype(BF16), eh)
        st_b = st_old.astype(BF16)
        acum_t = acum.T
        dt_t = dt.T
        row = lax.broadcasted_iota(jnp.int32, (L, L), 0)
        col = lax.broadcasted_iota(jnp.int32, (L, L), 1)
        mask = (row - col) * (1 - 2 * d) >= 0
        lane = lax.broadcasted_iota(jnp.int32, (L, LANES), 1)
        lo_half = lane < SSD_HEAD_DIM
        y_diag, y_off = [], []
        for g in range(SSD_GROUPS):
            cg = cm_b[:, g * SSD_STATE:(g + 1) * SSD_STATE]
            bg = bm_b[:, g * SSD_STATE:(g + 1) * SSD_STATE]
            scores = _dot_nt(cg, bg)
            y_off.append(_dot(cg, st_b[:, g * half:(g + 1) * half]))
            for j in range(SSD_HPG // 2):
                ws = []
                for hh in (g * SSD_HPG + 2 * j, g * SSD_HPG + 2 * j + 1):
                    seg = acum[:, hh:hh + 1] - acum_t[hh:hh + 1, :]
                    dec = jnp.exp(jnp.where(mask, seg, NEG_BIG))
                    ws.append(scores * dec * dt_t[hh:hh + 1, :])
                wp = jnp.concatenate(ws, axis=1).astype(BF16)
                p = g * (SSD_HPG // 2) + j
                slab = xs_b[:, p * LANES:(p + 1) * LANES]
                zero = jnp.zeros_like(slab)
                bd = jnp.concatenate([jnp.where(lo_half, slab, zero),
                                      jnp.where(lo_half, zero, slab)], axis=0)
                y_diag.append(_dot(wp, bd))
        y = (jnp.concatenate(y_diag, axis=1) + jnp.concatenate(y_off, axis=1) * eac_x
             + dsk_ref[0] * xs)
        y_ref[0, 0] = y.astype(BF16)


def _ssd_scan(xbc, dt, h0, consts, emit_y):
    b, s, _ = xbc.shape
    nc = s // SSD_CHUNK
    L = SSD_CHUNK
    tri, eh, a, dtb, dsk = consts
    chunk = lambda c, d: c + d * (nc - 1 - 2 * c)
    perdir = lambda width: pl.BlockSpec((1, 1, width), lambda bi, d, c: (d, 0, 0))
    state_spec = pl.BlockSpec((1, 1, SSD_STATE, SSD_INNER), lambda bi, d, c: (bi, d, 0, 0))
    out_specs = [state_spec]
    out_shape = [jax.ShapeDtypeStruct((b, 2, SSD_STATE, SSD_INNER), F32)]
    if emit_y:
        out_specs.insert(0, pl.BlockSpec((1, 1, L, SSD_INNER), lambda bi, d, c: (bi, d, chunk(c, d), 0)))
        out_shape.insert(0, jax.ShapeDtypeStruct((b, 2, s, SSD_INNER), BF16))
    return pl.pallas_call(
        functools.partial(_ssd_kernel, emit_y=emit_y),
        grid=(b, 2, nc),
        in_specs=[pl.BlockSpec((1, L, XBC_WIDTH), lambda bi, d, c: (bi, chunk(c, d), 0)),
                  pl.BlockSpec((1, 1, L, LANES), lambda bi, d, c: (bi, d, chunk(c, d), 0)),
                  state_spec,
                  pl.BlockSpec((1, L, L), lambda bi, d, c: (d, 0, 0)),
                  pl.BlockSpec(eh.shape, lambda bi, d, c: (0, 0)),
                  perdir(LANES), perdir(LANES), perdir(SSD_INNER)],
        out_specs=out_specs,
        out_shape=out_shape,
        scratch_shapes=[pltpu.VMEM((SSD_STATE, SSD_INNER), F32)],
        compiler_params=_params("parallel", "parallel", "arbitrary"),
        name="ssd_scan" if emit_y else "ssd_state",
    )(xbc, dt, h0, tri, eh, a, dtb, dsk)


def _attn_kernel(q_ref, k_ref, v_ref, lamv_ref, g_ref, o_ref, m_ref, acc_ref, s0_ref, s1_ref, *, tk, lam_init):
    tq = q_ref.shape[3]
    nk = k_ref.shape[1] // tk
    m_ref[...] = jnp.full(m_ref.shape, NEG_BIG, F32)
    acc_ref[...] = jnp.zeros(acc_ref.shape, F32)
    lane_tiles = [slice(t * LANES, (t + 1) * LANES) for t in range(tk // LANES)]

    def scores(j, s_ref):
        k = k_ref[0, pl.ds(pl.multiple_of(j * tk, tk), tk), :]
        s_ref[...] = _dot_nt(q_ref[0, 0].reshape(2 * tq, LANES), k)

    def accumulate(j, s_ref):
        v = v_ref[0, pl.ds(pl.multiple_of(j * tk, tk), tk), :]
        v_ones = jnp.concatenate([v, jnp.ones_like(v)], axis=1)
        tile_max = s_ref[:, lane_tiles[0]]
        for sl in lane_tiles[1:]:
            tile_max = jnp.maximum(tile_max, s_ref[:, sl])
        m_old = m_ref[...]
        m_new = jnp.maximum(m_old, jnp.max(tile_max, axis=-1, keepdims=True))
        alpha = jnp.exp2(m_old - m_new)
        p = jnp.concatenate([jnp.exp2(s_ref[:, sl] - m_new) for sl in lane_tiles], axis=1)
        m_ref[...] = m_new
        acc_ref[...] = jnp.concatenate([alpha, alpha], axis=1) * acc_ref[...] + _dot(p.astype(BF16), v_ones)

    scores(0, s0_ref)
    n_pairs = (nk - 1) // 2

    def pair(i, carry):
        j = 2 * i
        scores(j + 1, s1_ref)
        accumulate(j, s0_ref)
        scores(j + 2, s0_ref)
        accumulate(j + 1, s1_ref)
        return carry

    lax.fori_loop(0, n_pairs, pair, 0)
    if nk - 2 * n_pairs == 1:
        accumulate(nk - 1, s0_ref)
    else:
        scores(nk - 1, s1_ref)
        accumulate(nk - 2, s0_ref)
        accumulate(nk - 1, s1_ref)

    lv = lamv_ref[...]
    lam = (jnp.exp(jnp.sum(lv[0:1] * lv[1:2], keepdims=True))
           - jnp.exp(jnp.sum(lv[2:3] * lv[3:4], keepdims=True)) + lam_init)
    acc = acc_ref[...]
    o = acc[:, :LANES] / acc[:, LANES:]
    o = o[:tq] - lam * o[tq:]
    o = o * lax.rsqrt(jnp.mean(o * o, axis=-1, keepdims=True) + EPS) * g_ref[...]
    o_ref[0] = (o * (1.0 - lam_init)).astype(BF16)


def _attention(q, k, v, lamv, subln_g, lam_init):
    b, nh, _, s, _ = q.shape
    n_keys = k.shape[1]
    tq = min(512, s)
    tk = 768 if n_keys % 768 == 0 else 256
    return pl.pallas_call(
        functools.partial(_attn_kernel, tk=tk, lam_init=lam_init),
        grid=(b, nh, s // tq),
        in_specs=[pl.BlockSpec((1, 1, 2, tq, LANES), lambda bi, h, i: (bi, h, 0, i, 0)),
                  pl.BlockSpec((1, n_keys, LANES), lambda bi, h, i: (bi, 0, h)),
                  pl.BlockSpec((1, n_keys, LANES), lambda bi, h, i: (bi, 0, h)),
                  pl.BlockSpec((8, LANES), lambda bi, h, i: (0, 0)),
                  pl.BlockSpec((1, LANES), lambda bi, h, i: (0, 0))],
        out_specs=pl.BlockSpec((1, tq, LANES), lambda bi, h, i: (bi, i, h)),
        out_shape=jax.ShapeDtypeStruct((b, s, DA_INNER), BF16),
        scratch_shapes=[pltpu.VMEM((2 * tq, LANES), F32), pltpu.VMEM((2 * tq, 2 * LANES), F32),
                        pltpu.VMEM((2 * tq, tk), F32), pltpu.VMEM((2 * tq, tk), F32)],
        compiler_params=_params("parallel", "parallel", "parallel"),
        name="diff_attn",
    )(q, k, v, lamv, subln_g)


def _outproj_kernel(yf_ref, yb_ref, z_ref, o_ref, x_ref, gs_ref, wa_ref, wb_ref, gate_ref,
                    g2_ref, shift_ref, scale_ref, xl_ref, h2_ref):
    z = z_ref[0].astype(F32)
    t = (yf_ref[0, 0].astype(F32) + yb_ref[0, 0].astype(F32)) * _silu(z)
    ys = t * lax.rsqrt(jnp.mean(t * t, axis=-1, keepdims=True) + EPS) * gs_ref[...]
    mix = _dot(ys.astype(BF16), wa_ref[...]) + _dot(o_ref[0], wb_ref[...])
    xl = x_ref[0] + gate_ref[0] * mix
    xl_ref[0] = xl
    h2_ref[0] = _rmsnorm_mod(xl, g2_ref[...], shift_ref[0], scale_ref[0]).astype(BF16)


def _outproj(y, z, o, x, gs, wa, wb, gate, g2, shift, scale):
    b, s, d = x.shape
    tm = min(512, s)
    full = lambda a: pl.BlockSpec(a.shape, lambda bi, i: (0,) * a.ndim)
    row = lambda width: pl.BlockSpec((1, tm, width), lambda bi, i: (bi, i, 0))
    vec = pl.BlockSpec((1, 1, d), lambda bi, i: (bi, 0, 0))
    ydir = lambda dd: pl.BlockSpec((1, 1, tm, SSD_INNER), lambda bi, i: (bi, dd, i, 0))
    return pl.pallas_call(
        _outproj_kernel,
        grid=(b, s // tm),
        in_specs=[ydir(0), ydir(1), row(SSD_INNER), row(DA_INNER), row(d), full(gs), full(wa), full(wb),
                  vec, full(g2), vec, vec],
        out_specs=[row(d), row(d)],
        out_shape=[jax.ShapeDtypeStruct((b, s, d), F32), jax.ShapeDtypeStruct((b, s, d), BF16)],
        compiler_params=_params("parallel", "parallel"),
        name="outproj",
    )(y, y, z, o, x, gs, wa, wb, gate, g2, shift, scale)


def _ffn_up_kernel(h_ref, wv_ref, wg_ref, val_ref, gate_ref):
    h = h_ref[0]
    val_ref[0] = _dot(h, wv_ref[...]).astype(BF16)
    gate_ref[0] = _dot(h, wg_ref[...]).astype(BF16)


def _ffn_up(h, wv, wg):
    b, s, d = h.shape
    tm = min(512, s)
    full = lambda a: pl.BlockSpec(a.shape, lambda bi, i: (0,) * a.ndim)
    row = lambda width: pl.BlockSpec((1, tm, width), lambda bi, i: (bi, i, 0))
    return pl.pallas_call(
        _ffn_up_kernel,
        grid=(b, s // tm),
        in_specs=[row(d), full(wv), full(wg)],
        out_specs=[row(D_FF), row(D_FF)],
        out_shape=[jax.ShapeDtypeStruct((b, s, D_FF), BF16)] * 2,
        compiler_params=_params("parallel", "parallel"),
        name="ffn_up",
    )(h, wv, wg)


def _ffn_down_kernel(val_ref, cg_ref, xl_ref, w_ref, gate_ref, gf_ref, o_ref):
    u = (val_ref[0].astype(F32) * cg_ref[0].astype(F32)).astype(BF16)
    x = xl_ref[0] + gate_ref[0] * _dot(u, w_ref[...])
    o_ref[0] = x * lax.rsqrt(jnp.mean(x * x, axis=-1, keepdims=True) + EPS) * gf_ref[...]


def _ffn_down(val, cg, xl, w, gate, gf):
    b, s, d = xl.shape
    tm = min(512, s)
    full = lambda a: pl.BlockSpec(a.shape, lambda bi, i: (0,) * a.ndim)
    row = lambda width: pl.BlockSpec((1, tm, width), lambda bi, i: (bi, i, 0))
    vec = pl.BlockSpec((1, 1, d), lambda bi, i: (bi, 0, 0))
    return pl.pallas_call(
        _ffn_down_kernel,
        grid=(b, s // tm),
        in_specs=[row(D_FF), row(D_FF), row(d), full(w), vec, full(gf)],
        out_specs=row(d),
        out_shape=jax.ShapeDtypeStruct((b, s, d), F32),
        compiler_params=_params("parallel", "parallel"),
        name="ffn_down",
    )(val, cg, xl, w, gate, gf)


def _rope_tables(n):
    inv = jnp.power(ROPE_BASE, -jnp.arange(0, ROPE_AXIS_DIM, 2, dtype=F32) / ROPE_AXIS_DIM)
    t = jnp.arange(n)
    ar = (t // GRID_W).astype(F32)[:, None] * inv
    ac = (t % GRID_W).astype(F32)[:, None] * inv
    ang = jnp.concatenate([ar, ar, ac, ac] * 2, axis=-1)
    first = (jnp.arange(LANES) % ROPE_AXIS_DIM) < ROPE_AXIS_DIM // 2
    sin = jnp.sin(ang)
    return jnp.cos(ang), jnp.where(first, -sin, 0.0), jnp.where(first, 0.0, sin)


def kernel(x, c, ctx, c_ctx, w_mod, b_mod, norm1_g, w_in, conv_w, conv_b, a_log, dt_bias, d_skip,
           ssd_norm_g, lam_q1, lam_k1, lam_q2, lam_k2, subln_g, w_out, norm2_g, w_up, ffn_conv_w,
           ffn_conv_b, w_down, final_g):
    b, n, d = x.shape
    nctx = ctx.shape[1]
    lam_init = 0.8 - 0.6 * math.exp(-0.3 * 0)
    q_scale = DA_QK_DIM ** -0.5 * math.log2(math.e)

    cvec = jnp.concatenate([c, c_ctx[None], jnp.zeros((8 - b - 1, d), F32)], axis=0)
    mod = _mod(cvec, w_mod[0], b_mod[0])
    mod_l = [mod[:b, i * d:(i + 1) * d].reshape(b, 1, d) for i in range(6)]
    mod_c = [jnp.broadcast_to(mod[b, i * d:(i + 1) * d].reshape(1, 1, d), (b, 1, d)) for i in range(2)]

    wi = w_in[0].astype(BF16)
    o_x, o_dt, o_q = SSD_INNER, SSD_INNER + XBC_WIDTH, SSD_INNER + XBC_WIDTH + 2 * SSD_HEADS
    pad_dt = lambda a: jnp.pad(a, ((0, 0), (0, LANES - SSD_HEADS)))
    w_parts = (wi[:, :o_x], wi[:, o_x:o_dt],
               jnp.concatenate([pad_dt(wi[:, o_dt:o_dt + SSD_HEADS]), pad_dt(wi[:, o_dt + SSD_HEADS:o_q])], axis=1),
               wi[:, o_q:o_q + DA_INNER], wi[:, o_q + DA_INNER:o_q + 2 * DA_INNER], wi[:, o_q + 2 * DA_INNER:])
    g1 = norm1_g[0].reshape(1, d)
    rope_l = _rope_tables(n)
    rope_c = (jnp.ones((nctx, LANES), F32), jnp.zeros((nctx, LANES), F32), jnp.zeros((nctx, LANES), F32))
    z_l, xbc_l, dt_l, q_l, k_l, v_l = _inproj(x, g1, mod_l[0], mod_l[1], rope_l, w_parts, q_scale)
    _, xbc_c, dt_c, _, k_c, v_c = _inproj(ctx, g1, mod_c[0], mod_c[1], rope_c, w_parts, q_scale)

    xbc_l = _dwconv_silu(xbc_l, conv_w[0], conv_b[0])
    xbc_c = _dwconv_silu(xbc_c, conv_w[0], conv_b[0])
    idx = jnp.arange(SSD_CHUNK)
    tri = jnp.stack([idx[:, None] >= idx[None, :], idx[:, None] <= idx[None, :]]).astype(BF16)
    eh = (jnp.arange(LANES)[:, None] == (jnp.arange(SSD_INNER) // SSD_HEAD_DIM)[None, :]).astype(BF16)
    lanes16 = lambda a: jnp.pad(a.astype(F32), ((0, 0), (0, LANES - SSD_HEADS))).reshape(2, 1, LANES)
    consts = (tri, eh, lanes16(-jnp.exp(a_log[0].astype(F32))), lanes16(dt_bias[0]),
              jnp.repeat(d_skip[0].astype(F32), SSD_HEAD_DIM, axis=1).reshape(2, 1, SSD_INNER))
    h_zero = jnp.zeros((b, 2, SSD_STATE, SSD_INNER), F32)
    (h_ctx,) = _ssd_scan(xbc_c, dt_c, h_zero, consts, emit_y=False)
    y_l, _ = _ssd_scan(xbc_l, dt_l, h_ctx, consts, emit_y=True)

    k_all = jnp.concatenate([k_l, k_c], axis=1)
    v_all = jnp.concatenate([v_l, v_c], axis=1)
    lamv = jnp.zeros((8, LANES), F32).at[:4, :DA_QK_DIM].set(
        jnp.stack([lam_q1[0], lam_k1[0], lam_q2[0], lam_k2[0]]).astype(F32))
    o_l = _attention(q_l, k_all, v_all, lamv, subln_g[0].reshape(1, DA_V_DIM).astype(F32), lam_init)

    wo = w_out[0].astype(BF16)
    xl, h2 = _outproj(y_l, z_l, o_l, x, ssd_norm_g[0].reshape(1, SSD_INNER), wo[:SSD_INNER], wo[SSD_INNER:],
                      mod_l[2], norm2_g[0].reshape(1, d), mod_l[3], mod_l[4])
    wu = w_up[0].astype(BF16)
    val, gate = _ffn_up(h2, wu[:, :D_FF], wu[:, D_FF:])
    cg = _dwconv_silu(gate, ffn_conv_w[0], ffn_conv_b[0])
    return _ffn_down(val, cg, xl, w_down[0].astype(BF16), mod_l[5], final_g.reshape(1, d))
```

```python
import functools
import math

import jax
import jax.numpy as jnp
from jax import lax
from jax.experimental import pallas as pl
from jax.experimental.pallas import tpu as pltpu

F32 = jnp.float32
BF16 = jnp.bfloat16

D_MODEL = 1024
GRID_W = 64
EPS = 1e-6

SSD_HEADS = 16
SSD_HEAD_DIM = 64
SSD_INNER = SSD_HEADS * SSD_HEAD_DIM
SSD_GROUPS = 2
SSD_HPG = SSD_HEADS // SSD_GROUPS
SSD_STATE = 128
SSD_CHUNK = 128
BC_WIDTH = SSD_GROUPS * SSD_STATE
XBC_WIDTH = SSD_INNER + 2 * BC_WIDTH

DA_HEADS = 4
DA_QK_DIM = 64
DA_V_DIM = 2 * DA_QK_DIM
DA_INNER = DA_HEADS * DA_V_DIM
ROPE_BASE = 10000.0
ROPE_AXIS_DIM = DA_QK_DIM // 2

D_FF = 2816

LANES = 128
F32_SUBLANES = 8
BF16_SUBLANES = 16
VMEM_LIMIT = 56 * 1024 * 1024
NEG_BIG = -1e30

_NT = (((1,), (1,)), ((), ()))


def _dot(a, b):
    return jnp.dot(a, b, preferred_element_type=F32)


def _dot_nt(a, b):
    return lax.dot_general(a, b, _NT, preferred_element_type=F32)


def _split_bf16(v):
    hi = v.astype(BF16)
    lo = (v - hi.astype(F32)).astype(BF16)
    return hi, lo


def _dot_split(a, b, lhs_split):
    if lhs_split:
        return _dot(a[0], b) + _dot(a[1], b)
    return _dot(a, b[0]) + _dot(a, b[1])


def _silu(v):
    return v / (1.0 + jnp.exp(-v))


def _softplus(v):
    return jnp.maximum(v, 0.0) + jnp.log(1.0 + jnp.exp(-jnp.abs(v)))


def _params(*sem):
    return pltpu.CompilerParams(dimension_semantics=sem, vmem_limit_bytes=VMEM_LIMIT)


def _conv3_silu(xe, w_ref, b_ref, halo, t):
    n = xe.shape[0]
    mid = slice(halo, halo + t)
    w = w_ref[...]
    acc = (b_ref[...] + pltpu.roll(xe, 1, 0)[mid] * w[0:1] + xe[mid] * w[1:2]
           + pltpu.roll(xe, n - 1, 0)[mid] * w[2:3])
    return _silu(acc)


def _mod_kernel(c_ref, w_ref, b_ref, o_ref):
    a_hi, a_lo = _split_bf16(_silu(c_ref[...]))
    w_hi, w_lo = _split_bf16(w_ref[...])
    o_ref[...] = _dot(a_hi, w_hi) + _dot(a_lo, w_hi) + _dot(a_hi, w_lo) + b_ref[...]


def _mod(cvec, w_mod, b_mod):
    rows, d = cvec.shape
    n = w_mod.shape[1]
    tn = 1536
    return pl.pallas_call(
        _mod_kernel,
        grid=(n // tn,),
        in_specs=[pl.BlockSpec((rows, d), lambda j: (0, 0)),
                  pl.BlockSpec((d, tn), lambda j: (0, j)),
                  pl.BlockSpec((1, tn), lambda j: (0, j))],
        out_specs=pl.BlockSpec((rows, tn), lambda j: (0, j)),
        out_shape=jax.ShapeDtypeStruct((rows, n), F32),
        compiler_params=_params("arbitrary"),
        name="mod",
    )(cvec, w_mod, b_mod.reshape(1, n))


def _rmsnorm_mod(x, g, shift, scale):
    y = x * lax.rsqrt(jnp.mean(x * x, axis=-1, keepdims=True) + EPS) * g
    return y * (1.0 + scale) + shift


def _rope(t, cos, sin_a, sin_b):
    return (t * cos + pltpu.roll(t, LANES - ROPE_AXIS_DIM // 2, 1) * sin_a
            + pltpu.roll(t, ROPE_AXIS_DIM // 2, 1) * sin_b)


def _inproj_kernel(x_ref, xp_ref, xn_ref, g_ref, shift_ref, scale_ref, cos_ref, sa_ref, sb_ref,
                   wz_ref, wxbc_ref, wdt_ref, wq_ref, wk_ref, wv_ref, cw_ref, cb_ref,
                   z_ref, xbc_ref, bt_ref, dt_ref, dtt_ref, q_ref, k_ref, v_ref, *, q_scale):
    i = pl.program_id(1)
    last = pl.num_programs(1) - 1
    tm = x_ref.shape[1]
    hl = xp_ref.shape[1]
    norm = lambda v: _rmsnorm_mod(v, g_ref[...], shift_ref[0], scale_ref[0])
    hf = norm(x_ref[0])
    h = hf.astype(BF16)
    hp = norm(xp_ref[0]) * (i > 0).astype(F32)
    hn = norm(xn_ref[0]) * (i < last).astype(F32)
    hext = jnp.concatenate([hp, hf, hn], axis=0).astype(BF16)
    xbc = _conv3_silu(_dot(hext, wxbc_ref[...]), cw_ref, cb_ref, hl, tm)
    xbc_ref[0] = xbc.astype(BF16)
    bt_ref[0] = xbc[:, SSD_INNER:SSD_INNER + BC_WIDTH].T.astype(BF16)

    z_ref[0] = _dot(h, wz_ref[...]).astype(BF16)
    dt = _dot(h, wdt_ref[...])
    dt_ref[0] = dt
    dtt_ref[0] = dt.T[:2 * SSD_HEADS]
    v_ref[0] = _dot(h, wv_ref[...]).astype(BF16)
    cos, sa, sb = cos_ref[...], sa_ref[...], sb_ref[...]
    lane = lax.broadcasted_iota(jnp.int32, cos.shape, 1)
    first_map = lane < DA_QK_DIM
    q = _dot(h, wq_ref[...])
    k = _dot(h, wk_ref[...])
    for hd in range(DA_HEADS):
        sl = slice(hd * LANES, (hd + 1) * LANES)
        qh = _rope(q[:, sl], cos, sa, sb) * q_scale
        q_ref[0, hd, 0] = jnp.where(first_map, qh, 0.0).astype(BF16)
        q_ref[0, hd, 1] = jnp.where(first_map, 0.0, qh).astype(BF16)
        k_ref[0, :, sl] = _rope(k[:, sl], cos, sa, sb).astype(BF16)


def _inproj(x, g, shift, scale, rope, w, conv_w, conv_b, q_scale):
    b, s, d = x.shape
    tm = min(512, s)
    hl = F32_SUBLANES
    per, nh = tm // hl, s // hl
    cos, sa, sb = rope
    full = lambda a: pl.BlockSpec(a.shape, lambda bi, i: (0,) * a.ndim)
    row = lambda width: pl.BlockSpec((1, tm, width), lambda bi, i: (bi, i, 0))
    colmajor = lambda rows: pl.BlockSpec((1, rows, tm), lambda bi, i: (bi, 0, i))
    vec = pl.BlockSpec((1, 1, d), lambda bi, i: (bi, 0, 0))
    tab = pl.BlockSpec((tm, LANES), lambda bi, i: (i, 0))
    prev = pl.BlockSpec((1, hl, d), lambda bi, i: (bi, jnp.maximum(i * per - 1, 0), 0))
    nxt = pl.BlockSpec((1, hl, d), lambda bi, i: (bi, jnp.minimum((i + 1) * per, nh - 1), 0))
    return pl.pallas_call(
        functools.partial(_inproj_kernel, q_scale=q_scale),
        grid=(b, s // tm),
        in_specs=[row(d), prev, nxt, full(g), vec, vec, tab, tab, tab] + [full(a) for a in w]
                 + [full(conv_w), full(conv_b)],
        out_specs=[row(SSD_INNER), row(XBC_WIDTH), colmajor(BC_WIDTH), row(LANES), colmajor(2 * SSD_HEADS),
                   pl.BlockSpec((1, DA_HEADS, 2, tm, LANES), lambda bi, i: (bi, 0, 0, i, 0)),
                   row(DA_INNER), row(DA_INNER)],
        out_shape=[jax.ShapeDtypeStruct((b, s, SSD_INNER), BF16),
                   jax.ShapeDtypeStruct((b, s, XBC_WIDTH), BF16),
                   jax.ShapeDtypeStruct((b, BC_WIDTH, s), BF16),
                   jax.ShapeDtypeStruct((b, s, LANES), F32),
                   jax.ShapeDtypeStruct((b, 2 * SSD_HEADS, s), F32),
                   jax.ShapeDtypeStruct((b, DA_HEADS, 2, s, LANES), BF16),
                   jax.ShapeDtypeStruct((b, s, DA_INNER), BF16),
                   jax.ShapeDtypeStruct((b, s, DA_INNER), BF16)],
        compiler_params=_params("parallel", "parallel"),
        name="inproj",
    )(x, x, x, g, shift, scale, cos, sa, sb, *w, conv_w, conv_b)


def _ssd_kernel(xf_ref, xb_ref, btf_ref, btb_ref, dtf_ref, dtb_ref, dttf_ref, dttb_ref, h0_ref,
                tril_ref, triu_ref, eh_ref, acol_ref, bcol_ref, arow_ref, brow_ref, dsk_ref,
                *rest, emit_y):
    if emit_y:
        yf_ref, yb_ref, ht_ref, state_ref = rest
    else:
        ht_ref, state_ref = rest
    c = pl.program_id(1)
    nc = pl.num_programs(1)
    L = SSD_CHUNK
    nh = SSD_HEADS
    half = SSD_INNER // SSD_GROUPS

    @pl.when(c == 0)
    def _():
        state_ref[...] = h0_ref[0]

    tril = tril_ref[...]
    triu = triu_ref[...]
    lane = lax.broadcasted_iota(jnp.int32, (L, LANES), 1)
    fwd_lane = lane < nh

    dt_c = _softplus(jnp.where(fwd_lane, dtf_ref[0], dtb_ref[0]) + bcol_ref[...])
    da_c = dt_c * acol_ref[...]
    da_split = _split_bf16(da_c)
    acum_c = jnp.where(fwd_lane, _dot_split(tril, da_split, False), _dot_split(triu, da_split, False))
    total = jnp.sum(da_c, axis=0, keepdims=True)
    dte = jnp.exp(total - acum_c) * dt_c
    cdec = jnp.broadcast_to(jnp.exp(total), (F32_SUBLANES, LANES))
    eh = eh_ref[...]
    dte_x = _dot(dte.astype(BF16), eh)
    cdec_x = _dot_split(_split_bf16(cdec), eh, True)[0:1]

    if emit_y:
        eac_x = _dot(jnp.exp(acum_c).astype(BF16), eh)
        dt_r = _softplus(jnp.concatenate([dttf_ref[0][:nh], dttb_ref[0][nh:]], axis=0) + brow_ref[...])
        da_hi, da_lo = _split_bf16(dt_r * arow_ref[...])
        acum_r = jnp.concatenate([_dot_split((da_hi[:nh], da_lo[:nh]), triu, True),
                                  _dot_split((da_hi[nh:], da_lo[nh:]), tril, True)], axis=0)
        row = lax.broadcasted_iota(jnp.int32, (L, L), 0)
        col = lax.broadcasted_iota(jnp.int32, (L, L), 1)
        lo_half = lane < SSD_HEAD_DIM

    for d in range(2):
        xbc_b = (xf_ref, xb_ref)[d][0]
        bt = (btf_ref, btb_ref)[d][0]
        dsl = slice(d * SSD_INNER, (d + 1) * SSD_INNER)
        xs_b = xbc_b[:, :SSD_INNER]
        xs = xs_b.astype(F32)
        st_old = state_ref[d]
        xd = (xs * dte_x[:, dsl]).astype(BF16)
        st_new = [_dot(bt[g * SSD_STATE:(g + 1) * SSD_STATE], xd[:, g * half:(g + 1) * half])
                  for g in range(SSD_GROUPS)]
        state_ref[d] = cdec_x[:, dsl] * st_old + jnp.concatenate(st_new, axis=1)

        if emit_y:
            bm_b = xbc_b[:, SSD_INNER:SSD_INNER + BC_WIDTH]
            cm_b = xbc_b[:, SSD_INNER + BC_WIDTH:]
            st_b = st_old.astype(BF16)
            mask = (row >= col) if d == 0 else (row <= col)
            y_diag, y_off = [], []
            for g in range(SSD_GROUPS):
                cg = cm_b[:, g * SSD_STATE:(g + 1) * SSD_STATE]
                bg = bm_b[:, g * SSD_STATE:(g + 1) * SSD_STATE]
                scores = _dot_nt(cg, bg)
                y_off.append(_dot(cg, st_b[:, g * half:(g + 1) * half]))
                for j in range(SSD_HPG // 2):
                    ws = []
                    for hh in (g * SSD_HPG + 2 * j, g * SSD_HPG + 2 * j + 1):
                        ci = d * nh + hh
                        seg = acum_c[:, ci:ci + 1] - acum_r[ci:ci + 1, :]
                        dec = jnp.exp(jnp.where(mask, seg, NEG_BIG))
                        ws.append(scores * dec * dt_r[ci:ci + 1, :])
                    wp = jnp.concatenate(ws, axis=1).astype(BF16)
                    p = g * (SSD_HPG // 2) + j
                    slab = xs_b[:, p * LANES:(p + 1) * LANES]
                    zero = jnp.zeros_like(slab)
                    bd = jnp.concatenate([jnp.where(lo_half, slab, zero),
                                          jnp.where(lo_half, zero, slab)], axis=0)
                    y_diag.append(_dot(wp, bd))
            y = (jnp.concatenate(y_diag, axis=1) + jnp.concatenate(y_off, axis=1) * eac_x[:, dsl]
                 + dsk_ref[d:d + 1, :] * xs)
            (yf_ref, yb_ref)[d][0] = y.astype(BF16)

    @pl.when(c == nc - 1)
    def _():
        ht_ref[0] = state_ref[...]


def _ssd_scan(xbc, bt, dt, dtt, h0, consts, emit_y):
    b, s, _ = xbc.shape
    nc = s // SSD_CHUNK
    L = SSD_CHUNK
    full = lambda a: pl.BlockSpec(a.shape, lambda bi, c: (0,) * a.ndim)
    fwd_rows = lambda width: pl.BlockSpec((1, L, width), lambda bi, c: (bi, c, 0))
    bwd_rows = lambda width: pl.BlockSpec((1, L, width), lambda bi, c: (bi, nc - 1 - c, 0))
    fwd_cols = lambda rows: pl.BlockSpec((1, rows, L), lambda bi, c: (bi, 0, c))
    bwd_cols = lambda rows: pl.BlockSpec((1, rows, L), lambda bi, c: (bi, 0, nc - 1 - c))
    state_spec = pl.BlockSpec((1, 2, SSD_STATE, SSD_INNER), lambda bi, c: (bi, 0, 0, 0))
    out_specs = [state_spec]
    out_shape = [jax.ShapeDtypeStruct((b, 2, SSD_STATE, SSD_INNER), F32)]
    if emit_y:
        out_specs = [fwd_rows(SSD_INNER), bwd_rows(SSD_INNER)] + out_specs
        out_shape = [jax.ShapeDtypeStruct((b, s, SSD_INNER), BF16)] * 2 + out_shape
    return pl.pallas_call(
        functools.partial(_ssd_kernel, emit_y=emit_y),
        grid=(b, nc),
        in_specs=[fwd_rows(XBC_WIDTH), bwd_rows(XBC_WIDTH), fwd_cols(BC_WIDTH), bwd_cols(BC_WIDTH),
                  fwd_rows(LANES), bwd_rows(LANES), fwd_cols(2 * SSD_HEADS), bwd_cols(2 * SSD_HEADS),
                  state_spec] + [full(a) for a in consts],
        out_specs=out_specs,
        out_shape=out_shape,
        scratch_shapes=[pltpu.VMEM((2, SSD_STATE, SSD_INNER), F32)],
        compiler_params=_params("parallel", "arbitrary"),
        name="ssd_scan" if emit_y else "ssd_state",
    )(xbc, xbc, bt, bt, dt, dt, dtt, dtt, h0, *consts)


def _attn_kernel(q_ref, k_ref, v_ref, lamv_ref, g_ref, o_ref, m_ref, acc_ref, s0_ref, s1_ref, *, tk, lam_init):
    tq = q_ref.shape[3]
    nk = k_ref.shape[1] // tk
    m_ref[...] = jnp.full(m_ref.shape, NEG_BIG, F32)
    acc_ref[...] = jnp.zeros(acc_ref.shape, F32)
    lane_tiles = [slice(t * LANES, (t + 1) * LANES) for t in range(tk // LANES)]

    def scores(j, s_ref):
        s_ref[...] = _dot_nt(q_ref[0, 0].reshape(2 * tq, LANES), k_ref[0, j * tk:(j + 1) * tk, :])

    def accumulate(j, s_ref):
        v = v_ref[0, j * tk:(j + 1) * tk, :]
        v_ones = jnp.concatenate([v, jnp.ones_like(v)], axis=1)
        tile_max = s_ref[:, lane_tiles[0]]
        for sl in lane_tiles[1:]:
            tile_max = jnp.maximum(tile_max, s_ref[:, sl])
        m_old = m_ref[...]
        m_new = jnp.maximum(m_old, jnp.max(tile_max, axis=-1, keepdims=True))
        alpha = jnp.exp2(m_old - m_new)
        p = jnp.concatenate([jnp.exp2(s_ref[:, sl] - m_new) for sl in lane_tiles], axis=1)
        m_ref[...] = m_new
        acc_ref[...] = jnp.concatenate([alpha, alpha], axis=1) * acc_ref[...] + _dot(p.astype(BF16), v_ones)

    bufs = (s0_ref, s1_ref)
    scores(0, s0_ref)
    for j in range(nk):
        if j + 1 < nk:
            scores(j + 1, bufs[(j + 1) % 2])
        accumulate(j, bufs[j % 2])

    lv = lamv_ref[...]
    lam = (jnp.exp(jnp.sum(lv[0:1] * lv[1:2], keepdims=True))
           - jnp.exp(jnp.sum(lv[2:3] * lv[3:4], keepdims=True)) + lam_init)
    acc = acc_ref[...]
    o = acc[:, :LANES] / acc[:, LANES:]
    o = o[:tq] - lam * o[tq:]
    o = o * lax.rsqrt(jnp.mean(o * o, axis=-1, keepdims=True) + EPS) * g_ref[...]
    o_ref[0] = (o * (1.0 - lam_init)).astype(BF16)


def _attention(q, k, v, lamv, subln_g, lam_init):
    b, nh, _, s, _ = q.shape
    n_keys = k.shape[1]
    tq = min(512, s)
    tk = 768 if n_keys % 768 == 0 else 256
    return pl.pallas_call(
        functools.partial(_attn_kernel, tk=tk, lam_init=lam_init),
        grid=(b, nh, s // tq),
        in_specs=[pl.BlockSpec((1, 1, 2, tq, LANES), lambda bi, h, i: (bi, h, 0, i, 0)),
                  pl.BlockSpec((1, n_keys, LANES), lambda bi, h, i: (bi, 0, h)),
                  pl.BlockSpec((1, n_keys, LANES), lambda bi, h, i: (bi, 0, h)),
                  pl.BlockSpec((8, LANES), lambda bi, h, i: (0, 0)),
                  pl.BlockSpec((1, LANES), lambda bi, h, i: (0, 0))],
        out_specs=pl.BlockSpec((1, tq, LANES), lambda bi, h, i: (bi, i, h)),
        out_shape=jax.ShapeDtypeStruct((b, s, DA_INNER), BF16),
        scratch_shapes=[pltpu.VMEM((2 * tq, LANES), F32), pltpu.VMEM((2 * tq, 2 * LANES), F32),
                        pltpu.VMEM((2 * tq, tk), F32), pltpu.VMEM((2 * tq, tk), F32)],
        compiler_params=_params("parallel", "parallel", "parallel"),
        name="diff_attn",
    )(q, k, v, lamv, subln_g)


def _outproj_kernel(yf_ref, yb_ref, z_ref, o_ref, x_ref, gs_ref, wa_ref, wb_ref, gate_ref,
                    g2_ref, shift_ref, scale_ref, xl_ref, h2_ref):
    z = z_ref[0].astype(F32)
    t = (yf_ref[0].astype(F32) + yb_ref[0].astype(F32)) * _silu(z)
    ys = t * lax.rsqrt(jnp.mean(t * t, axis=-1, keepdims=True) + EPS) * gs_ref[...]
    mix = _dot(ys.astype(BF16), wa_ref[...]) + _dot(o_ref[0], wb_ref[...])
    xl = x_ref[0] + gate_ref[0] * mix
    xl_ref[0] = xl
    h2_ref[0] = _rmsnorm_mod(xl, g2_ref[...], shift_ref[0], scale_ref[0]).astype(BF16)


def _outproj(yf, yb, z, o, x, gs, wa, wb, gate, g2, shift, scale):
    b, s, d = x.shape
    tm = min(512, s)
    full = lambda a: pl.BlockSpec(a.shape, lambda bi, i: (0,) * a.ndim)
    row = lambda width: pl.BlockSpec((1, tm, width), lambda bi, i: (bi, i, 0))
    vec = pl.BlockSpec((1, 1, d), lambda bi, i: (bi, 0, 0))
    return pl.pallas_call(
        _outproj_kernel,
        grid=(b, s // tm),
        in_specs=[row(SSD_INNER), row(SSD_INNER), row(SSD_INNER), row(DA_INNER), row(d), full(gs), full(wa),
                  full(wb), vec, full(g2), vec, vec],
        out_specs=[row(d), row(d)],
        out_shape=[jax.ShapeDtypeStruct((b, s, d), F32), jax.ShapeDtypeStruct((b, s, d), BF16)],
        compiler_params=_params("parallel", "parallel"),
        name="outproj",
    )(yf, yb, z, o, x, gs, wa, wb, gate, g2, shift, scale)


FFN_COL_CHUNKS = 2


def _ffn_up_kernel(h_ref, hp_ref, hn_ref, wv_ref, wg_ref, cw_ref, cb_ref, u_ref):
    i = pl.program_id(1)
    last = pl.num_programs(1) - 1
    t = h_ref.shape[1]
    hl = hp_ref.shape[1]
    h = h_ref[0]
    hp = (hp_ref[0].astype(F32) * (i > 0).astype(F32)).astype(BF16)
    hn = (hn_ref[0].astype(F32) * (i < last).astype(F32)).astype(BF16)
    hext = jnp.concatenate([hp, h, hn], axis=0)
    width = D_FF // FFN_COL_CHUNKS
    for cc in range(FFN_COL_CHUNKS):
        sl = slice(cc * width, (cc + 1) * width)
        act = _conv3_silu(_dot(hext, wg_ref[:, sl]), cw_ref.at[:, sl], cb_ref.at[:, sl], hl, t)
        u_ref[0, :, sl] = (act * _dot(h, wv_ref[:, sl])).astype(BF16)


def _ffn_up(h, wv, wg, conv_w, conv_b):
    b, s, d = h.shape
    tm = min(512, s)
    hl = BF16_SUBLANES
    per, nh = tm // hl, s // hl
    full = lambda a: pl.BlockSpec(a.shape, lambda bi, i: (0,) * a.ndim)
    row = lambda width: pl.BlockSpec((1, tm, width), lambda bi, i: (bi, i, 0))
    prev = pl.BlockSpec((1, hl, d), lambda bi, i: (bi, jnp.maximum(i * per - 1, 0), 0))
    nxt = pl.BlockSpec((1, hl, d), lambda bi, i: (bi, jnp.minimum((i + 1) * per, nh - 1), 0))
    return pl.pallas_call(
        _ffn_up_kernel,
        grid=(b, s // tm),
        in_specs=[row(d), prev, nxt, full(wv), full(wg), full(conv_w), full(conv_b)],
        out_specs=row(D_FF),
        out_shape=jax.ShapeDtypeStruct((b, s, D_FF), BF16),
        compiler_params=_params("parallel", "parallel"),
        name="ffn_up",
    )(h, h, h, wv, wg, conv_w, conv_b)


def _ffn_down_kernel(u_ref, xl_ref, w_ref, gate_ref, gf_ref, o_ref):
    x = xl_ref[0] + gate_ref[0] * _dot(u_ref[0], w_ref[...])
    o_ref[0] = x * lax.rsqrt(jnp.mean(x * x, axis=-1, keepdims=True) + EPS) * gf_ref[...]


def _ffn_down(u, xl, w, gate, gf):
    b, s, d = xl.shape
    tm = min(512, s)
    full = lambda a: pl.BlockSpec(a.shape, lambda bi, i: (0,) * a.ndim)
    row = lambda width: pl.BlockSpec((1, tm, width), lambda bi, i: (bi, i, 0))
    vec = pl.BlockSpec((1, 1, d), lambda bi, i: (bi, 0, 0))
    return pl.pallas_call(
        _ffn_down_kernel,
        grid=(b, s // tm),
        in_specs=[row(D_FF), row(d), full(w), vec, full(gf)],
        out_specs=row(d),
        out_shape=jax.ShapeDtypeStruct((b, s, d), F32),
        compiler_params=_params("parallel", "parallel"),
        name="ffn_down",
    )(u, xl, w, gate, gf)


def _rope_tables(n):
    inv = jnp.power(ROPE_BASE, -jnp.arange(0, ROPE_AXIS_DIM, 2, dtype=F32) / ROPE_AXIS_DIM)
    t = jnp.arange(n)
    ar = (t // GRID_W).astype(F32)[:, None] * inv
    ac = (t % GRID_W).astype(F32)[:, None] * inv
    ang = jnp.concatenate([ar, ar, ac, ac] * 2, axis=-1)
    first = (jnp.arange(LANES) % ROPE_AXIS_DIM) < ROPE_AXIS_DIM // 2
    sin = jnp.sin(ang)
    return jnp.cos(ang), jnp.where(first, -sin, 0.0), jnp.where(first, 0.0, sin)


def _conv_params(w, bias):
    c = w.shape[1]
    return (jnp.pad(w.astype(F32), ((0, F32_SUBLANES - w.shape[0]), (0, 0))), bias.astype(F32).reshape(1, c))


def _ssd_consts(a_log, dt_bias, d_skip):
    idx = jnp.arange(SSD_CHUNK)
    tril = (idx[None, :] <= idx[:, None]).astype(BF16)
    triu = (idx[None, :] >= idx[:, None]).astype(BF16)
    out_head = jnp.arange(2 * SSD_INNER) // SSD_HEAD_DIM
    eh = (jnp.arange(LANES)[:, None] == out_head[None, :]).astype(BF16)
    a = -jnp.exp(a_log.astype(F32)).reshape(2 * SSD_HEADS)
    bias = dt_bias.astype(F32).reshape(2 * SSD_HEADS)
    col = lambda v: jnp.pad(v, (0, LANES - 2 * SSD_HEADS)).reshape(1, LANES)
    rowf = lambda v: jnp.broadcast_to(v[:, None], (2 * SSD_HEADS, SSD_CHUNK))
    dsk = jnp.repeat(d_skip.astype(F32), SSD_HEAD_DIM, axis=1)
    return (tril, triu, eh, col(a), col(bias), rowf(a), rowf(bias), dsk)


def kernel(x, c, ctx, c_ctx, w_mod, b_mod, norm1_g, w_in, conv_w, conv_b, a_log, dt_bias, d_skip,
           ssd_norm_g, lam_q1, lam_k1, lam_q2, lam_k2, subln_g, w_out, norm2_g, w_up, ffn_conv_w,
           ffn_conv_b, w_down, final_g):
    b, n, d = x.shape
    nctx = ctx.shape[1]
    lam_init = 0.8 - 0.6 * math.exp(-0.3 * 0)
    q_scale = DA_QK_DIM ** -0.5 * math.log2(math.e)

    cvec = jnp.concatenate([c, c_ctx[None], jnp.zeros((8 - b - 1, d), F32)], axis=0)
    mod = _mod(cvec, w_mod[0], b_mod[0])
    mod_l = [mod[:b, i * d:(i + 1) * d].reshape(b, 1, d) for i in range(6)]
    mod_c = [jnp.broadcast_to(mod[b, i * d:(i + 1) * d].reshape(1, 1, d), (b, 1, d)) for i in range(2)]

    wi = w_in[0].astype(BF16)
    o_x, o_dt, o_q = SSD_INNER, SSD_INNER + XBC_WIDTH, SSD_INNER + XBC_WIDTH + 2 * SSD_HEADS
    w_parts = (wi[:, :o_x], wi[:, o_x:o_dt], jnp.pad(wi[:, o_dt:o_q], ((0, 0), (0, LANES - 2 * SSD_HEADS))),
               wi[:, o_q:o_q + DA_INNER], wi[:, o_q + DA_INNER:o_q + 2 * DA_INNER], wi[:, o_q + 2 * DA_INNER:])
    g1 = norm1_g[0].reshape(1, d)
    cw, cb = _conv_params(conv_w[0], conv_b[0])
    rope_l = _rope_tables(n)
    rope_c = (jnp.ones((nctx, LANES), F32), jnp.zeros((nctx, LANES), F32), jnp.zeros((nctx, LANES), F32))
    z_l, xbc_l, bt_l, dt_l, dtt_l, q_l, k_l, v_l = _inproj(x, g1, mod_l[0], mod_l[1], rope_l, w_parts, cw, cb, q_scale)
    _, xbc_c, bt_c, dt_c, dtt_c, _, k_c, v_c = _inproj(ctx, g1, mod_c[0], mod_c[1], rope_c, w_parts, cw, cb, q_scale)

    consts = _ssd_consts(a_log[0], dt_bias[0], d_skip[0])
    h_zero = jnp.zeros((b, 2, SSD_STATE, SSD_INNER), F32)
    (h_ctx,) = _ssd_scan(xbc_c, bt_c, dt_c, dtt_c, h_zero, consts, emit_y=False)
    yf, yb, _ = _ssd_scan(xbc_l, bt_l, dt_l, dtt_l, h_ctx, consts, emit_y=True)

    k_all = jnp.concatenate([k_l, k_c], axis=1)
    v_all = jnp.concatenate([v_l, v_c], axis=1)
    lamv = jnp.zeros((8, LANES), F32).at[:4, :DA_QK_DIM].set(
        jnp.stack([lam_q1[0], lam_k1[0], lam_q2[0], lam_k2[0]]).astype(F32))
    o_l = _attention(q_l, k_all, v_all, lamv, subln_g[0].reshape(1, DA_V_DIM).astype(F32), lam_init)

    wo = w_out[0].astype(BF16)
    xl, h2 = _outproj(yf, yb, z_l, o_l, x, ssd_norm_g[0].reshape(1, SSD_INNER), wo[:SSD_INNER], wo[SSD_INNER:],
                      mod_l[2], norm2_g[0].reshape(1, d), mod_l[3], mod_l[4])
    wu = w_up[0].astype(BF16)
    fw, fb = _conv_params(ffn_conv_w[0], ffn_conv_b[0])
    u = _ffn_up(h2, wu[:, :D_FF], wu[:, D_FF:], fw, fb)
    return _ffn_down(u, xl, w_down[0].astype(BF16), mod_l[5], final_g.reshape(1, d))
```
